```python
import jax, jax.numpy as jnp
from jax import lax
import numpy as np

D_MODEL = 1024
BATCH = 4
SEQ = 4096
DEPTH = 1

PLE_DIM = 256
RET_HEADS = 4
RET_QK_DIM = 256
RET_V_DIM = 512
RET_CHUNK = 128
MLA_HEADS = 8
MLA_NOPE_DIM = 128
MLA_ROPE_DIM = 64
MLA_QK_DIM = MLA_NOPE_DIM + MLA_ROPE_DIM
MLA_V_DIM = 128
MLA_Q_LORA = 384
MLA_KV_LORA = 256
ATTN_BLOCK = 128
D_FF = 4 * D_MODEL
ROPE_BASE = 10000.0
RMS_EPS = 1e-6

RET_QK_W = RET_HEADS * RET_QK_DIM
RET_V_W = RET_HEADS * RET_V_DIM
MLA_V_W = MLA_HEADS * MLA_V_DIM
IN_SIZES = (RET_QK_W, RET_QK_W, RET_V_W, RET_V_W, MLA_Q_LORA, MLA_KV_LORA, MLA_ROPE_DIM, D_MODEL, D_MODEL)
IN_WIDTH = sum(IN_SIZES)

kernel_name = "hybrid_retention_mla_gated_block"


def rms_norm(x, gain):
    xf = x.astype(jnp.float32)
    y = xf * lax.rsqrt(jnp.mean(xf * xf, axis=-1, keepdims=True) + RMS_EPS)
    return (y * gain.astype(jnp.float32)).astype(x.dtype)


def rope(x, pos):
    half = x.shape[-1] // 2
    inv = ROPE_BASE ** (-jnp.arange(half, dtype=jnp.float32) / half)
    ang = pos.astype(jnp.float32)[:, :, None] * inv
    cos = jnp.cos(ang)[:, :, None, :]
    sin = jnp.sin(ang)[:, :, None, :]
    xf = x.astype(jnp.float32)
    x1, x2 = xf[..., :half], xf[..., half:]
    return jnp.concatenate([x1 * cos - x2 * sin, x2 * cos + x1 * sin], axis=-1).astype(x.dtype)


def retention(q, k, v):
    B, S, H, dk = q.shape
    dv = v.shape[-1]
    C = RET_CHUNK
    N = S // C
    log_g = jnp.log1p(-jnp.exp2(-5.0 - jnp.arange(H, dtype=jnp.float32)))
    idx = jnp.arange(C, dtype=jnp.float32)
    diff = idx[:, None] - idx[None, :]
    decay_in = jnp.where(diff >= 0, jnp.exp(jnp.maximum(diff, 0.0)[None] * log_g[:, None, None]), 0.0)
    q_dec = jnp.exp((idx + 1.0)[None, :] * log_g[:, None])
    k_dec = jnp.exp((C - 1.0 - idx)[None, :] * log_g[:, None])
    chunk_dec = jnp.exp(C * log_g)

    def to_chunks(t):
        return t.astype(jnp.float32).reshape(B, N, C, H, t.shape[-1]).transpose(1, 0, 3, 2, 4)

    def step(state, inp):
        qc, kc, vc = inp
        scores = jnp.einsum('bhnd,bhmd->bhnm', qc, kc) * decay_in
        o = (jnp.einsum('bhnm,bhmv->bhnv', scores, vc)
             + jnp.einsum('bhnd,bhdv->bhnv', qc, state) * q_dec[None, :, :, None])
        state = (state * chunk_dec[None, :, None, None]
                 + jnp.einsum('bhmd,bhmv->bhdv', kc * k_dec[None, :, :, None], vc))
        return state, o

    state0 = jnp.zeros((B, H, dk, dv), jnp.float32)
    _, o = lax.scan(step, state0, (to_chunks(q), to_chunks(k), to_chunks(v)))
    return o.transpose(1, 0, 3, 2, 4).reshape(B, S, H, dv).astype(v.dtype)


def causal_block_attention(q, k, v):
    B, S, H, D = q.shape
    nb = S // ATTN_BLOCK
    scale = D ** -0.5
    qb = q.reshape(B, nb, ATTN_BLOCK, H, D).transpose(1, 0, 2, 3, 4)
    key_pos = jnp.arange(S)

    def one(args):
        qi, bi = args
        s = jnp.einsum('bqhd,bkhd->bhqk', qi, k).astype(jnp.float32) * scale
        q_pos = bi * ATTN_BLOCK + jnp.arange(ATTN_BLOCK)
        s = jnp.where(key_pos[None, :] <= q_pos[:, None], s, -1e30)
        pr = jax.nn.softmax(s, axis=-1).astype(v.dtype)
        return jnp.einsum('bhqk,bkhv->bqhv', pr, v)

    o = lax.map(one, (qb, jnp.arange(nb)))
    return o.transpose(1, 0, 2, 3, 4).reshape(B, S, H, v.shape[-1])


def hybrid_layer(x, p_l, pos, norm_mix, w_in, ret_norm, q_lat_norm, kv_lat_norm, w_uq, w_ukv,
                 q_norm, k_norm, w_br, w_bm, w_o, norm_mlp, w_up, w_down, norm_ple, w_ple_gate, w_ple):
    B, S, _ = x.shape
    h = rms_norm(x, norm_mix)
    proj = h @ w_in
    offsets = np.cumsum(IN_SIZES)[:-1].tolist()
    rq, rk, rv, rg, cq, ckv, krope, gr, gm = jnp.split(proj, offsets, axis=-1)

    rq = rope(rq.reshape(B, S, RET_HEADS, RET_QK_DIM), pos)
    rk = rope(rk.reshape(B, S, RET_HEADS, RET_QK_DIM), pos) * (RET_QK_DIM ** -0.5)
    rv = rv.reshape(B, S, RET_HEADS, RET_V_DIM)
    ro = retention(rq, rk, rv)
    ro = rms_norm(ro, ret_norm.reshape(RET_HEADS, RET_V_DIM)).reshape(B, S, RET_V_W)
    a_ret = (ro * jax.nn.silu(rg)) @ w_br

    cq = rms_norm(cq, q_lat_norm)
    q = (cq @ w_uq).reshape(B, S, MLA_HEADS, MLA_QK_DIM)
    ckv = rms_norm(ckv, kv_lat_norm)
    kv = (ckv @ w_ukv).reshape(B, S, MLA_HEADS, MLA_NOPE_DIM + MLA_V_DIM)
    k_nope, v = kv[..., :MLA_NOPE_DIM], kv[..., MLA_NOPE_DIM:]
    k_rope = jnp.broadcast_to(krope[:, :, None, :], (B, S, MLA_HEADS, MLA_ROPE_DIM))
    k = jnp.concatenate([k_nope, k_rope], axis=-1)
    q = rms_norm(q, q_norm)
    k = rms_norm(k, k_norm)
    q = jnp.concatenate([q[..., :MLA_NOPE_DIM], rope(q[..., MLA_NOPE_DIM:], pos)], axis=-1)
    k = jnp.concatenate([k[..., :MLA_NOPE_DIM], rope(k[..., MLA_NOPE_DIM:], pos)], axis=-1)
    mo = causal_block_attention(q, k, v).reshape(B, S, MLA_V_W)
    a_mla = mo @ w_bm

    mixed = jax.nn.sigmoid(gr) * a_ret + jax.nn.sigmoid(gm) * a_mla
    x = x + mixed @ w_o

    h2 = rms_norm(x, norm_mlp)
    x = x + jnp.square(jax.nn.relu(h2 @ w_up)) @ w_down

    gate = jax.nn.sigmoid(rms_norm(x, norm_ple) @ w_ple_gate)
    x = x + gate * (p_l @ w_ple)
    return x


def setup_inputs(seed: int = 0) -> dict:
    key = jax.random.key(seed)
    ks = jax.random.split(key, 24)

    def dense(k, shape, fan_in):
        return jax.random.normal(k, shape, jnp.float32) * (fan_in ** -0.5)

    def gain(k, n):
        return 1.0 + 0.05 * jax.random.normal(k, (DEPTH, n), jnp.float32)

    x = jax.random.normal(ks[0], (BATCH, SEQ, D_MODEL), jnp.float32)
    p = jax.random.normal(ks[1], (DEPTH, BATCH, SEQ, PLE_DIM), jnp.float32)
    offset = jax.random.randint(ks[2], (BATCH, 1), 0, 1024, dtype=jnp.int32)
    positions = offset + jnp.arange(SEQ, dtype=jnp.int32)[None, :]
    return {
        "x": x,
        "p": p,
        "positions": positions,
        "norm_mix": gain(ks[3], D_MODEL),
        "w_in": dense(ks[4], (DEPTH, D_MODEL, IN_WIDTH), D_MODEL),
        "ret_norm": gain(ks[5], RET_V_W),
        "q_lat_norm": gain(ks[6], MLA_Q_LORA),
        "kv_lat_norm": gain(ks[7], MLA_KV_LORA),
        "w_uq": dense(ks[8], (DEPTH, MLA_Q_LORA, MLA_HEADS * MLA_QK_DIM), MLA_Q_LORA),
        "w_ukv": dense(ks[9], (DEPTH, MLA_KV_LORA, MLA_HEADS * (MLA_NOPE_DIM + MLA_V_DIM)), MLA_KV_LORA),
        "q_norm": gain(ks[10], MLA_QK_DIM),
        "k_norm": gain(ks[11], MLA_QK_DIM),
        "w_br": dense(ks[12], (DEPTH, RET_V_W, D_MODEL), RET_V_W),
        "w_bm": dense(ks[13], (DEPTH, MLA_V_W, D_MODEL), MLA_V_W),
        "w_o": dense(ks[14], (DEPTH, D_MODEL, D_MODEL), D_MODEL),
        "norm_mlp": gain(ks[15], D_MODEL),
        "w_up": dense(ks[16], (DEPTH, D_MODEL, D_FF), D_MODEL),
        "w_down": dense(ks[17], (DEPTH, D_FF, D_MODEL), D_FF),
        "norm_ple": gain(ks[18], D_MODEL),
        "w_ple_gate": dense(ks[19], (DEPTH, D_MODEL, D_MODEL), D_MODEL),
        "w_ple": dense(ks[20], (DEPTH, PLE_DIM, D_MODEL), PLE_DIM),
    }


def reference(x, p, positions, norm_mix, w_in, ret_norm, q_lat_norm, kv_lat_norm, w_uq, w_ukv,
              q_norm, k_norm, w_br, w_bm, w_o, norm_mlp, w_up, w_down, norm_ple, w_ple_gate, w_ple):
    for i in range(DEPTH):
        x = hybrid_layer(x, p[i], positions, norm_mix[i], w_in[i], ret_norm[i], q_lat_norm[i],
                         kv_lat_norm[i], w_uq[i], w_ukv[i], q_norm[i], k_norm[i], w_br[i], w_bm[i],
                         w_o[i], norm_mlp[i], w_up[i], w_down[i], norm_ple[i], w_ple_gate[i], w_ple[i])
    return x
```

```python
import functools

import jax
import jax.numpy as jnp
from jax import lax
from jax.experimental import pallas as pl
from jax.experimental.pallas import tpu as pltpu

D_MODEL = 1024
PLE_DIM = 256
RET_HEADS = 4
RET_QK_DIM = 256
RET_V_DIM = 512
RET_CHUNK = 128
MLA_HEADS = 8
MLA_NOPE_DIM = 128
MLA_ROPE_DIM = 64
MLA_QK_DIM = MLA_NOPE_DIM + MLA_ROPE_DIM
MLA_V_DIM = 128
MLA_Q_LORA = 384
MLA_KV_LORA = 256
D_FF = 4 * D_MODEL
ROPE_BASE = 10000.0
RMS_EPS = 1e-6

RET_QK_W = RET_HEADS * RET_QK_DIM
RET_V_W = RET_HEADS * RET_V_DIM
MLA_V_W = MLA_HEADS * MLA_V_DIM
MLA_PAD_DIM = 256
MLA_QK_W = MLA_HEADS * MLA_PAD_DIM

LANES = 128
VMEM_LIMIT = 56 * 1024 * 1024

F32 = jnp.float32
BF16 = jnp.bfloat16


def _params(*sem):
    return pltpu.CompilerParams(dimension_semantics=sem, vmem_limit_bytes=VMEM_LIMIT)


def _full(shape):
    nd = len(shape)
    return pl.BlockSpec(shape, lambda *_: (0,) * nd)


def _rms(x, gain):
    ms = jnp.mean(x * x, axis=-1, keepdims=True)
    return x * lax.rsqrt(ms + RMS_EPS) * gain


def _dot(a, b):
    return jnp.dot(a, b, preferred_element_type=F32)


def _dot_nt(a, b):
    return lax.dot_general(a, b, (((1,), (1,)), ((), ())), preferred_element_type=F32)


def _dot_tn(a, b):
    return lax.dot_general(a, b, (((0,), (0,)), ((), ())), preferred_element_type=F32)


def _rope_tables_kernel(pos_ref, inv_r_ref, inv_m_ref, mask_ref, cr_ref, sr_ref, cm_ref, sm_ref):
    pos = pos_ref[...]
    ang = pos * inv_r_ref[...]
    cr_ref[...] = jnp.cos(ang)
    sr_ref[...] = jnp.sin(ang)
    ang_m = pos * inv_m_ref[...]
    mask = mask_ref[...]
    cm_ref[...] = jnp.cos(ang_m) * mask
    sm_ref[...] = jnp.sin(ang_m) * mask


def _rope_tables(pos_f, bm=2048):
    T = pos_f.shape[0]
    half_r = RET_QK_DIM // 2
    half_m = MLA_ROPE_DIM // 2
    inv_r = ROPE_BASE ** (-jnp.arange(half_r, dtype=F32) / half_r)
    inv_m = ROPE_BASE ** (-jnp.arange(half_m, dtype=F32) / half_m)
    zeros = jnp.zeros((LANES - 2 * half_m,), F32)
    inv_m2 = jnp.concatenate([inv_m, inv_m, zeros])[None, :]
    mask = jnp.concatenate([jnp.ones((2 * half_m,), F32), zeros])[None, :]
    row = pl.BlockSpec((bm, LANES), lambda i: (i, 0))
    out = jax.ShapeDtypeStruct((T, LANES), F32)
    return pl.pallas_call(
        _rope_tables_kernel,
        grid=(T // bm,),
        in_specs=[pl.BlockSpec((bm, 1), lambda i: (i, 0)), _full((1, LANES)), _full((1, LANES)), _full((1, LANES))],
        out_specs=[row, row, row, row],
        out_shape=[out, out, out, out],
        compiler_params=_params("parallel"),
        name="rope_tables",
    )(pos_f, inv_r[None, :], inv_m2, mask)


def _proj_qk_kernel(x_ref, gain_ref, w_ref, cos_ref, sin_ref, o_ref, h_ref):
    j = pl.program_id(1)

    @pl.when(j == 0)
    def _():
        h_ref[...] = _rms(x_ref[...], gain_ref[...]).astype(BF16)

    acc = _dot(h_ref[...], w_ref[...])
    scale = jnp.where(j == 1, RET_QK_DIM ** -0.5, 1.0).astype(F32)
    cos = cos_ref[...]
    sin = sin_ref[...]
    half = RET_QK_DIM // 2
    for hd in range(RET_HEADS):
        lo = hd * RET_QK_DIM
        x1 = acc[:, lo:lo + half]
        x2 = acc[:, lo + half:lo + 2 * half]
        o_ref[:, lo:lo + half] = ((x1 * cos - x2 * sin) * scale).astype(BF16)
        o_ref[:, lo + half:lo + 2 * half] = ((x2 * cos + x1 * sin) * scale).astype(BF16)


def _proj_qk(x2d, gain, w, cos_r, sin_r, bm=1024):
    T = x2d.shape[0]
    bn = RET_QK_W
    return pl.pallas_call(
        _proj_qk_kernel,
        grid=(T // bm, 2),
        in_specs=[
            pl.BlockSpec((bm, D_MODEL), lambda i, j: (i, 0)),
            _full((1, D_MODEL)),
            pl.BlockSpec((D_MODEL, bn), lambda i, j: (0, j)),
            pl.BlockSpec((bm, LANES), lambda i, j: (i, 0)),
            pl.BlockSpec((bm, LANES), lambda i, j: (i, 0)),
        ],
        out_specs=pl.BlockSpec((bm, bn), lambda i, j: (i, j)),
        out_shape=jax.ShapeDtypeStruct((T, 2 * bn), BF16),
        scratch_shapes=[pltpu.VMEM((bm, D_MODEL), BF16)],
        compiler_params=_params("parallel", "arbitrary"),
        name="proj_ret_qk",
    )(x2d, gain, w, cos_r, sin_r)


def _proj_act_kernel(x_ref, gain_ref, w_ref, o_ref, h_ref):
    j = pl.program_id(1)

    @pl.when(j == 0)
    def _():
        h_ref[...] = _rms(x_ref[...], gain_ref[...]).astype(BF16)

    acc = _dot(h_ref[...], w_ref[...])

    @pl.when(j < 2)
    def _():
        o_ref[...] = acc.astype(BF16)

    @pl.when(jnp.logical_and(j >= 2, j < 4))
    def _():
        o_ref[...] = (acc * jax.nn.sigmoid(acc)).astype(BF16)

    @pl.when(j >= 4)
    def _():
        o_ref[...] = jax.nn.sigmoid(acc).astype(BF16)


def _proj_act(x2d, gain, w, bm=1024, bn=1024):
    T = x2d.shape[0]
    n = w.shape[1]
    return pl.pallas_call(
        _proj_act_kernel,
        grid=(T // bm, n // bn),
        in_specs=[
            pl.BlockSpec((bm, D_MODEL), lambda i, j: (i, 0)),
            _full((1, D_MODEL)),
            pl.BlockSpec((D_MODEL, bn), lambda i, j: (0, j)),
        ],
        out_specs=pl.BlockSpec((bm, bn), lambda i, j: (i, j)),
        out_shape=jax.ShapeDtypeStruct((T, n), BF16),
        scratch_shapes=[pltpu.VMEM((bm, D_MODEL), BF16)],
        compiler_params=_params("parallel", "arbitrary"),
        name="proj_v_gates",
    )(x2d, gain, w)


def _proj_lat_kernel(x_ref, gain_ref, w_ref, gq_ref, gkv_ref, cq_ref, ckv_ref, kr_ref):
    h = _rms(x_ref[...], gain_ref[...]).astype(BF16)
    acc = _dot(h, w_ref[...])
    cq_ref[...] = _rms(acc[:, :MLA_Q_LORA], gq_ref[...]).astype(BF16)
    ckv_ref[...] = _rms(acc[:, MLA_Q_LORA:MLA_Q_LORA + MLA_KV_LORA], gkv_ref[...]).astype(BF16)
    kr_ref[...] = acc[:, MLA_Q_LORA + MLA_KV_LORA:]


def _proj_lat(x2d, gain, w, gq, gkv, bm=1024):
    T = x2d.shape[0]
    n = w.shape[1]
    return pl.pallas_call(
        _proj_lat_kernel,
        grid=(T // bm,),
        in_specs=[
            pl.BlockSpec((bm, D_MODEL), lambda i: (i, 0)),
            _full((1, D_MODEL)),
            _full((D_MODEL, n)),
            _full((1, MLA_Q_LORA)),
            _full((1, MLA_KV_LORA)),
        ],
        out_specs=[
            pl.BlockSpec((bm, MLA_Q_LORA), lambda i: (i, 0)),
            pl.BlockSpec((bm, MLA_KV_LORA), lambda i: (i, 0)),
            pl.BlockSpec((bm, 2 * MLA_ROPE_DIM), lambda i: (i, 0)),
        ],
        out_shape=[
            jax.ShapeDtypeStruct((T, MLA_Q_LORA), BF16),
            jax.ShapeDtypeStruct((T, MLA_KV_LORA), BF16),
            jax.ShapeDtypeStruct((T, 2 * MLA_ROPE_DIM), F32),
        ],
        compiler_params=_params("parallel"),
        name="proj_latents",
    )(x2d, gain, w, gq, gkv)


def _retention_kernel(q_ref, k_ref, v_ref, g_ref, dec_ref, qd_ref, kd_ref, cd_ref, gain_ref, o_ref, state_ref, *, tb):
    @pl.when(pl.program_id(2) == 0)
    def _():
        state_ref[...] = jnp.zeros_like(state_ref)

    dec = dec_ref[...]
    qd = qd_ref[...]
    kd = kd_ref[...]
    cd = cd_ref[0:1, 0:1]
    gain = gain_ref[...]
    C = RET_CHUNK

    def body(c, carry):
        r0 = pl.multiple_of(c * C, C)
        q = q_ref[pl.ds(r0, C), :]
        k = k_ref[pl.ds(r0, C), :]
        v = v_ref[pl.ds(r0, C), :]
        st = state_ref[...]
        p = (_dot_nt(q, k) * dec).astype(BF16)
        o = _dot(p, v) + _dot(q, st.astype(BF16)) * qd
        kk = (k.astype(F32) * kd).astype(BF16)
        state_ref[...] = st * cd + _dot_tn(kk, v)
        on = _rms(o, gain)
        o_ref[pl.ds(r0, C), :] = (on * g_ref[pl.ds(r0, C), :].astype(F32)).astype(BF16)
        return carry

    lax.fori_loop(0, tb // C, body, 0)


def _retention(qk, vg, ret_gain, B, S, tb=1024):
    T = qk.shape[0]
    H, C, dk, dv = RET_HEADS, RET_CHUNK, RET_QK_DIM, RET_V_DIM
    nt = S // tb
    log_g = jnp.log1p(-jnp.exp2(-5.0 - jnp.arange(H, dtype=F32)))
    idx = jnp.arange(C, dtype=F32)
    diff = idx[:, None] - idx[None, :]
    decay_in = jnp.where(diff >= 0, jnp.exp(jnp.maximum(diff, 0.0)[None] * log_g[:, None, None]), 0.0)
    q_dec = jnp.exp((idx + 1.0)[None, :] * log_g[:, None])
    k_dec = jnp.exp((C - 1.0 - idx)[None, :] * log_g[:, None])
    chunk_dec = jnp.exp(C * log_g)
    qd = jnp.broadcast_to(q_dec[:, :, None], (H, C, dv))
    kd = jnp.broadcast_to(k_dec[:, :, None], (H, C, dk))
    cd = jnp.broadcast_to(chunk_dec[:, None, None], (H, 8, LANES))

    def tab(d1, d2):
        return pl.BlockSpec((None, d1, d2), lambda b, h, t: (h, 0, 0))

    return pl.pallas_call(
        functools.partial(_retention_kernel, tb=tb),
        grid=(B, H, nt),
        in_specs=[
            pl.BlockSpec((tb, dk), lambda b, h, t: (b * nt + t, h)),
            pl.BlockSpec((tb, dk), lambda b, h, t: (b * nt + t, H + h)),
            pl.BlockSpec((tb, dv), lambda b, h, t: (b * nt + t, h)),
            pl.BlockSpec((tb, dv), lambda b, h, t: (b * nt + t, H + h)),
            tab(C, C), tab(C, dv), tab(C, dk), tab(8, LANES),
            pl.BlockSpec((1, dv), lambda b, h, t: (0, h)),
        ],
        out_specs=pl.BlockSpec((tb, dv), lambda b, h, t: (b * nt + t, h)),
        out_shape=jax.ShapeDtypeStruct((T, RET_V_W), BF16),
        scratch_shapes=[pltpu.VMEM((dk, dv), F32)],
        compiler_params=_params("parallel", "parallel", "arbitrary"),
        name="retention",
    )(qk, qk, vg, vg, decay_in, qd, kd, cd, ret_gain)


def _mla_qkv_kernel(cq_ref, ckv_ref, kr_ref, cm_ref, sm_ref, wq_ref, wkn_ref, wv_ref,
                    gq_ref, gkn_ref, gkr_ref, q_ref, k_ref, v_ref):
    ckv = ckv_ref[...]
    yq = _dot(cq_ref[...], wq_ref[...])
    ykn = _dot(ckv, wkn_ref[...])
    v_ref[...] = _dot(ckv, wv_ref[...]).astype(BF16)
    cm = cm_ref[...]
    sm = sm_ref[...]
    first = (lax.broadcasted_iota(jnp.int32, (1, LANES), 1) < MLA_ROPE_DIM).astype(F32)
    inv_d = 1.0 / MLA_QK_DIM
    half = MLA_PAD_DIM // 2

    kr = kr_ref[...]
    ss_kr = jnp.sum(jnp.square(kr * first), axis=-1, keepdims=True)
    krg = kr * gkr_ref[...]
    k_roped = krg * cm + pltpu.roll(krg, MLA_ROPE_DIM, 1) * sm

    gq1 = gq_ref[:, :half]
    gq2 = gq_ref[:, half:]
    gkn = gkn_ref[...]
    for h in range(MLA_HEADS):
        lo = h * MLA_PAD_DIM
        y1 = yq[:, lo:lo + half]
        y2 = yq[:, lo + half:lo + 2 * half]
        ss = jnp.sum(y1 * y1, axis=-1, keepdims=True) + jnp.sum(jnp.square(y2 * first), axis=-1, keepdims=True)
        sc = lax.rsqrt(ss * inv_d + RMS_EPS) * (MLA_QK_DIM ** -0.5)
        y2g = y2 * gq2 * sc
        q_ref[:, lo:lo + half] = (y1 * gq1 * sc).astype(BF16)
        q_ref[:, lo + half:lo + 2 * half] = (y2g * cm + pltpu.roll(y2g, MLA_ROPE_DIM, 1) * sm).astype(BF16)
        kn = ykn[:, h * half:(h + 1) * half]
        ssk = jnp.sum(kn * kn, axis=-1, keepdims=True) + ss_kr
        sck = lax.rsqrt(ssk * inv_d + RMS_EPS)
        k_ref[:, lo:lo + half] = (kn * gkn * sck).astype(BF16)
        k_ref[:, lo + half:lo + 2 * half] = (k_roped * sck).astype(BF16)


def _mla_qkv(cq, ckv, kr, cm, sm, wq, wkn, wv, gq, gkn, gkr, bm=512):
    T = cq.shape[0]

    def row(w):
        return pl.BlockSpec((bm, w), lambda i: (i, 0))

    return pl.pallas_call(
        _mla_qkv_kernel,
        grid=(T // bm,),
        in_specs=[row(MLA_Q_LORA), row(MLA_KV_LORA), row(LANES), row(LANES), row(LANES),
                  _full(wq.shape), _full(wkn.shape), _full(wv.shape),
                  _full(gq.shape), _full(gkn.shape), _full(gkr.shape)],
        out_specs=[row(MLA_QK_W), row(MLA_QK_W), row(MLA_V_W)],
        out_shape=[jax.ShapeDtypeStruct((T, MLA_QK_W), BF16),
                   jax.ShapeDtypeStruct((T, MLA_QK_W), BF16),
                   jax.ShapeDtypeStruct((T, MLA_V_W), BF16)],
        compiler_params=_params("parallel"),
        name="mla_qkv",
    )(cq, ckv, kr, cm, sm, wq, wkn, wv, gq, gkn, gkr)


def _attn_kernel(q_ref, k_ref, v_ref, o_ref, m_ref, l_ref, acc_ref, *, tq):
    qi = pl.program_id(2)
    q = q_ref[...]
    m_ref[...] = jnp.full_like(m_ref, -1e30)
    l_ref[...] = jnp.zeros_like(l_ref)
    acc_ref[...] = jnp.zeros_like(acc_ref)

    def block(kj, masked):
        r0 = pl.multiple_of(kj * tq, tq)
        s = _dot_nt(q, k_ref[pl.ds(r0, tq), :])
        if masked:
            row = lax.broadcasted_iota(jnp.int32, (tq, tq), 0)
            col = lax.broadcasted_iota(jnp.int32, (tq, tq), 1)
            s = jnp.where(col <= row, s, -1e30)
        m_prev = m_ref[...]
        m_new = jnp.maximum(m_prev, jnp.max(s, axis=-1, keepdims=True))
        alpha = jnp.exp(m_prev - m_new)
        p = jnp.exp(s - m_new)
        l_ref[...] = alpha * l_ref[...] + jnp.sum(p, axis=-1, keepdims=True)
        acc_ref[...] = alpha * acc_ref[...] + _dot(p.astype(BF16), v_ref[pl.ds(r0, tq), :])
        m_ref[...] = m_new

    def body(kj, carry):
        block(kj, False)
        return carry

    lax.fori_loop(0, qi, body, 0)
    block(qi, True)
    o_ref[...] = (acc_ref[...] / l_ref[...]).astype(BF16)


def _attention(q, k, v, B, S, tq=512):
    T = q.shape[0]
    nq = S // tq
    return pl.pallas_call(
        functools.partial(_attn_kernel, tq=tq),
        grid=(B, MLA_HEADS, nq),
        in_specs=[
            pl.BlockSpec((tq, MLA_PAD_DIM), lambda b, h, i: (b * nq + i, h)),
            pl.BlockSpec((S, MLA_PAD_DIM), lambda b, h, i: (b, h)),
            pl.BlockSpec((S, MLA_V_DIM), lambda b, h, i: (b, h)),
        ],
        out_specs=pl.BlockSpec((tq, MLA_V_DIM), lambda b, h, i: (b * nq + i, h)),
        out_shape=jax.ShapeDtypeStruct((T, MLA_V_W), BF16),
        scratch_shapes=[pltpu.VMEM((tq, 1), F32), pltpu.VMEM((tq, 1), F32), pltpu.VMEM((tq, MLA_V_DIM), F32)],
        compiler_params=_params("parallel", "parallel", "arbitrary"),
        name="mla_attention",
    )(q, k, v)


def _merge_kernel(x_ref, ro_ref, mo_ref, gr_ref, gm_ref, wbr_ref, wbm_ref, wo_ref, o_ref):
    a_ret = _dot(ro_ref[...], wbr_ref[...])
    a_mla = _dot(mo_ref[...], wbm_ref[...])
    mixed = gr_ref[...].astype(F32) * a_ret + gm_ref[...].astype(F32) * a_mla
    o_ref[...] = x_ref[...] + _dot(mixed.astype(BF16), wo_ref[...])


def _merge(x2d, ro, mo, vg, wbr, wbm, wo, bm=512):
    T = x2d.shape[0]
    gate0 = (2 * RET_V_W) // D_MODEL
    return pl.pallas_call(
        _merge_kernel,
        grid=(T // bm,),
        in_specs=[
            pl.BlockSpec((bm, D_MODEL), lambda i: (i, 0)),
            pl.BlockSpec((bm, RET_V_W), lambda i: (i, 0)),
            pl.BlockSpec((bm, MLA_V_W), lambda i: (i, 0)),
            pl.BlockSpec((bm, D_MODEL), lambda i: (i, gate0)),
            pl.BlockSpec((bm, D_MODEL), lambda i: (i, gate0 + 1)),
            _full(wbr.shape), _full(wbm.shape), _full(wo.shape),
        ],
        out_specs=pl.BlockSpec((bm, D_MODEL), lambda i: (i, 0)),
        out_shape=jax.ShapeDtypeStruct((T, D_MODEL), F32),
        compiler_params=_params("parallel"),
        name="merge_out_proj",
    )(x2d, ro, mo, vg, vg, wbr, wbm, wo)


def _mlp_kernel(x_ref, gain_ref, wup_ref, wdown_ref, o_ref, *, bf):
    x = x_ref[...]
    h = _rms(x, gain_ref[...]).astype(BF16)
    acc = x
    for c in range(D_FF // bf):
        up = _dot(h, wup_ref[:, c * bf:(c + 1) * bf])
        a = jnp.square(jnp.maximum(up, 0.0)).astype(BF16)
        acc = acc + _dot(a, wdown_ref[c * bf:(c + 1) * bf, :])
    o_ref[...] = acc


def _mlp(x2d, gain, wup, wdown, bm=512, bf=1024):
    T = x2d.shape[0]
    return pl.pallas_call(
        functools.partial(_mlp_kernel, bf=bf),
        grid=(T // bm,),
        in_specs=[pl.BlockSpec((bm, D_MODEL), lambda i: (i, 0)), _full((1, D_MODEL)),
                  _full(wup.shape), _full(wdown.shape)],
        out_specs=pl.BlockSpec((bm, D_MODEL), lambda i: (i, 0)),
        out_shape=jax.ShapeDtypeStruct((T, D_MODEL), F32),
        compiler_params=_params("parallel"),
        name="mlp",
    )(x2d, gain, wup, wdown)


def _ple_kernel(x_ref, p_ref, gain_ref, wg_ref, wp_ref, o_ref):
    x = x_ref[...]
    h = _rms(x, gain_ref[...]).astype(BF16)
    gate = jax.nn.sigmoid(_dot(h, wg_ref[...]))
    o_ref[...] = x + gate * _dot(p_ref[...].astype(BF16), wp_ref[...])


def _ple(x2d, p2d, gain, wg, wp, bm=1024):
    T = x2d.shape[0]
    return pl.pallas_call(
        _ple_kernel,
        grid=(T // bm,),
        in_specs=[pl.BlockSpec((bm, D_MODEL), lambda i: (i, 0)), pl.BlockSpec((bm, PLE_DIM), lambda i: (i, 0)),
                  _full((1, D_MODEL)), _full(wg.shape), _full(wp.shape)],
        out_specs=pl.BlockSpec((bm, D_MODEL), lambda i: (i, 0)),
        out_shape=jax.ShapeDtypeStruct((T, D_MODEL), F32),
        compiler_params=_params("parallel"),
        name="ple",
    )(x2d, p2d, gain, wg, wp)


def _rotate_half_cols(w):
    half = w.shape[-1] // 2
    return jnp.concatenate([-w[..., half:], w[..., :half]], axis=-1)


def _swap_halves(g):
    half = g.shape[-1] // 2
    return jnp.concatenate([g[..., half:], g[..., :half]], axis=-1)


def _layer(x2d, p2d, tables, B, S, norm_mix, w_in, ret_norm, q_lat_norm, kv_lat_norm, w_uq, w_ukv,
           q_norm, k_norm, w_br, w_bm, w_o, norm_mlp, w_up, w_down, norm_ple, w_ple_gate, w_ple):
    cos_r, sin_r, cos_m, sin_m = tables
    o_rv = 2 * RET_QK_W
    o_cq = o_rv + 2 * RET_V_W
    o_ckv = o_cq + MLA_Q_LORA
    o_kr = o_ckv + MLA_KV_LORA
    o_gr = o_kr + MLA_ROPE_DIM

    gain_mix = norm_mix[None, :]
    w_qk = w_in[:, :o_rv].astype(BF16)
    w_vg = jnp.concatenate([w_in[:, o_rv:o_cq], w_in[:, o_gr:]], axis=1).astype(BF16)
    w_kr = w_in[:, o_kr:o_gr]
    w_lat = jnp.concatenate([w_in[:, o_cq:o_gr], _rotate_half_cols(w_kr)], axis=1).astype(BF16)

    qk = _proj_qk(x2d, gain_mix, w_qk, cos_r, sin_r)
    vg = _proj_act(x2d, gain_mix, w_vg)
    cq, ckv, kr = _proj_lat(x2d, gain_mix, w_lat, q_lat_norm[None, :], kv_lat_norm[None, :])

    ro = _retention(qk, vg, ret_norm[None, :], B, S)

    wq = w_uq.reshape(MLA_Q_LORA, MLA_HEADS, MLA_QK_DIM)
    wq_rope = wq[:, :, MLA_NOPE_DIM:]
    wq = jnp.concatenate([wq, _rotate_half_cols(wq_rope)], axis=-1).reshape(MLA_Q_LORA, MLA_QK_W).astype(BF16)
    wkv = w_ukv.reshape(MLA_KV_LORA, MLA_HEADS, MLA_NOPE_DIM + MLA_V_DIM)
    wkn = wkv[:, :, :MLA_NOPE_DIM].reshape(MLA_KV_LORA, MLA_HEADS * MLA_NOPE_DIM).astype(BF16)
    wv = wkv[:, :, MLA_NOPE_DIM:].reshape(MLA_KV_LORA, MLA_V_W).astype(BF16)
    gq = jnp.concatenate([q_norm, _swap_halves(q_norm[MLA_NOPE_DIM:])])[None, :]
    gkn = k_norm[None, :MLA_NOPE_DIM]
    gkr = jnp.concatenate([k_norm[MLA_NOPE_DIM:], _swap_halves(k_norm[MLA_NOPE_DIM:])])[None, :]
    q, k, v = _mla_qkv(cq, ckv, kr, cos_m, sin_m, wq, wkn, wv, gq, gkn, gkr)
    mo = _attention(q, k, v, B, S)

    x2d = _merge(x2d, ro, mo, vg, w_br.astype(BF16), w_bm.astype(BF16), w_o.astype(BF16))
    x2d = _mlp(x2d, norm_mlp[None, :], w_up.astype(BF16), w_down.astype(BF16))
    x2d = _ple(x2d, p2d, norm_ple[None, :], w_ple_gate.astype(BF16), w_ple.astype(BF16))
    return x2d


def kernel(x, p, positions, norm_mix, w_in, ret_norm, q_lat_norm, kv_lat_norm, w_uq, w_ukv, q_norm, k_norm,
           w_br, w_bm, w_o, norm_mlp, w_up, w_down, norm_ple, w_ple_gate, w_ple):
    B, S, D = x.shape
    T = B * S
    x2d = x.reshape(T, D)
    pos_f = positions.astype(F32).reshape(T, 1)
    tables = _rope_tables(pos_f)
    for i in range(p.shape[0]):
        x2d = _layer(x2d, p[i].reshape(T, PLE_DIM), tables, B, S, norm_mix[i], w_in[i], ret_norm[i],
                     q_lat_norm[i], kv_lat_norm[i], w_uq[i], w_ukv[i], q_norm[i], k_norm[i], w_br[i], w_bm[i],
                     w_o[i], norm_mlp[i], w_up[i], w_down[i], norm_ple[i], w_ple_gate[i], w_ple[i])
    return x2d.reshape(B, S, D)
```

```python
import functools

import jax
import jax.numpy as jnp
from jax import lax
from jax.experimental import pallas as pl
from jax.experimental.pallas import tpu as pltpu

D_MODEL = 1024
PLE_DIM = 256
RET_HEADS = 4
RET_QK_DIM = 256
RET_V_DIM = 512
RET_CHUNK = 128
MLA_HEADS = 8
MLA_NOPE_DIM = 128
MLA_ROPE_DIM = 64
MLA_QK_DIM = MLA_NOPE_DIM + MLA_ROPE_DIM
MLA_V_DIM = 128
MLA_Q_LORA = 384
MLA_KV_LORA = 256
D_FF = 4 * D_MODEL
ROPE_BASE = 10000.0
RMS_EPS = 1e-6

RET_QK_W = RET_HEADS * RET_QK_DIM
RET_V_W = RET_HEADS * RET_V_DIM
MLA_V_W = MLA_HEADS * MLA_V_DIM
MLA_PAD_DIM = 256
MLA_QK_W = MLA_HEADS * MLA_PAD_DIM

LANES = 128
VMEM_LIMIT = 56 * 1024 * 1024

F32 = jnp.float32
BF16 = jnp.bfloat16


def _params(*sem):
    return pltpu.CompilerParams(dimension_semantics=sem, vmem_limit_bytes=VMEM_LIMIT)


def _full(shape):
    nd = len(shape)
    return pl.BlockSpec(shape, lambda *_: (0,) * nd)


def _rms(x, gain):
    ms = jnp.mean(x * x, axis=-1, keepdims=True)
    return x * lax.rsqrt(ms + RMS_EPS) * gain


def _dot(a, b):
    return jnp.dot(a, b, preferred_element_type=F32)


def _dot_nt(a, b):
    return lax.dot_general(a, b, (((1,), (1,)), ((), ())), preferred_element_type=F32)


def _dot_tn(a, b):
    return lax.dot_general(a, b, (((0,), (0,)), ((), ())), preferred_element_type=F32)


def _rope_tables_kernel(pos_ref, inv_r_ref, inv_m_ref, mask_ref, cr_ref, sr_ref, cm_ref, sm_ref):
    pos = pos_ref[...]
    ang = pos * inv_r_ref[...]
    cr_ref[...] = jnp.cos(ang)
    sr_ref[...] = jnp.sin(ang)
    ang_m = pos * inv_m_ref[...]
    mask = mask_ref[...]
    cm_ref[...] = jnp.cos(ang_m) * mask
    sm_ref[...] = jnp.sin(ang_m) * mask


def _rope_tables(pos_f, bm=2048):
    T = pos_f.shape[0]
    half_r = RET_QK_DIM // 2
    half_m = MLA_ROPE_DIM // 2
    inv_r = ROPE_BASE ** (-jnp.arange(half_r, dtype=F32) / half_r)
    inv_m = ROPE_BASE ** (-jnp.arange(half_m, dtype=F32) / half_m)
    zeros = jnp.zeros((LANES - 2 * half_m,), F32)
    inv_m2 = jnp.concatenate([inv_m, inv_m, zeros])[None, :]
    mask = jnp.concatenate([jnp.ones((2 * half_m,), F32), zeros])[None, :]
    row = pl.BlockSpec((bm, LANES), lambda i: (i, 0))
    out = jax.ShapeDtypeStruct((T, LANES), F32)
    return pl.pallas_call(
        _rope_tables_kernel,
        grid=(T // bm,),
        in_specs=[pl.BlockSpec((bm, 1), lambda i: (i, 0)), _full((1, LANES)), _full((1, LANES)), _full((1, LANES))],
        out_specs=[row, row, row, row],
        out_shape=[out, out, out, out],
        compiler_params=_params("parallel"),
        name="rope_tables",
    )(pos_f, inv_r[None, :], inv_m2, mask)


def _proj_qk_kernel(x_ref, gain_ref, w_ref, cos_ref, sin_ref, o_ref, h_ref):
    j = pl.program_id(1)

    @pl.when(j == 0)
    def _():
        h_ref[...] = _rms(x_ref[...], gain_ref[...]).astype(BF16)

    acc = _dot(h_ref[...], w_ref[...])
    scale = jnp.where(j == 1, RET_QK_DIM ** -0.5, 1.0).astype(F32)
    cos = cos_ref[...]
    sin = sin_ref[...]
    half = RET_QK_DIM // 2
    for hd in range(RET_HEADS):
        lo = hd * RET_QK_DIM
        x1 = acc[:, lo:lo + half]
        x2 = acc[:, lo + half:lo + 2 * half]
        o_ref[:, lo:lo + half] = ((x1 * cos - x2 * sin) * scale).astype(BF16)
        o_ref[:, lo + half:lo + 2 * half] = ((x2 * cos + x1 * sin) * scale).astype(BF16)


def _proj_qk(x2d, gain, w, cos_r, sin_r, bm=1024):
    T = x2d.shape[0]
    bn = RET_QK_W
    return pl.pallas_call(
        _proj_qk_kernel,
        grid=(T // bm, 2),
        in_specs=[
            pl.BlockSpec((bm, D_MODEL), lambda i, j: (i, 0)),
            _full((1, D_MODEL)),
            pl.BlockSpec((D_MODEL, bn), lambda i, j: (0, j)),
            pl.BlockSpec((bm, LANES), lambda i, j: (i, 0)),
            pl.BlockSpec((bm, LANES), lambda i, j: (i, 0)),
        ],
        out_specs=pl.BlockSpec((bm, bn), lambda i, j: (i, j)),
        out_shape=jax.ShapeDtypeStruct((T, 2 * bn), BF16),
        scratch_shapes=[pltpu.VMEM((bm, D_MODEL), BF16)],
        compiler_params=_params("parallel", "arbitrary"),
        name="proj_ret_qk",
    )(x2d, gain, w, cos_r, sin_r)


def _proj_act_kernel(x_ref, gain_ref, w_ref, o_ref, h_ref):
    j = pl.program_id(1)

    @pl.when(j == 0)
    def _():
        h_ref[...] = _rms(x_ref[...], gain_ref[...]).astype(BF16)

    acc = _dot(h_ref[...], w_ref[...])

    @pl.when(j < 2)
    def _():
        o_ref[...] = acc.astype(BF16)

    @pl.when(jnp.logical_and(j >= 2, j < 4))
    def _():
        o_ref[...] = (acc * jax.nn.sigmoid(acc)).astype(BF16)

    @pl.when(j >= 4)
    def _():
        o_ref[...] = jax.nn.sigmoid(acc).astype(BF16)


def _proj_act(x2d, gain, w, bm=1024, bn=1024):
    T = x2d.shape[0]
    n = w.shape[1]
    return pl.pallas_call(
        _proj_act_kernel,
        grid=(T // bm, n // bn),
        in_specs=[
            pl.BlockSpec((bm, D_MODEL), lambda i, j: (i, 0)),
            _full((1, D_MODEL)),
            pl.BlockSpec((D_MODEL, bn), lambda i, j: (0, j)),
        ],
        out_specs=pl.BlockSpec((bm, bn), lambda i, j: (i, j)),
        out_shape=jax.ShapeDtypeStruct((T, n), BF16),
        scratch_shapes=[pltpu.VMEM((bm, D_MODEL), BF16)],
        compiler_params=_params("parallel", "arbitrary"),
        name="proj_v_gates",
    )(x2d, gain, w)


def _proj_lat_kernel(x_ref, gain_ref, w_ref, gq_ref, gkv_ref, cq_ref, ckv_ref, kr_ref):
    h = _rms(x_ref[...], gain_ref[...]).astype(BF16)
    acc = _dot(h, w_ref[...])
    cq_ref[...] = _rms(acc[:, :MLA_Q_LORA], gq_ref[...]).astype(BF16)
    ckv_ref[...] = _rms(acc[:, MLA_Q_LORA:MLA_Q_LORA + MLA_KV_LORA], gkv_ref[...]).astype(BF16)
    kr_ref[...] = acc[:, MLA_Q_LORA + MLA_KV_LORA:]


def _proj_lat(x2d, gain, w, gq, gkv, bm=1024):
    T = x2d.shape[0]
    n = w.shape[1]
    return pl.pallas_call(
        _proj_lat_kernel,
        grid=(T // bm,),
        in_specs=[
            pl.BlockSpec((bm, D_MODEL), lambda i: (i, 0)),
            _full((1, D_MODEL)),
            _full((D_MODEL, n)),
            _full((1, MLA_Q_LORA)),
            _full((1, MLA_KV_LORA)),
        ],
        out_specs=[
            pl.BlockSpec((bm, MLA_Q_LORA), lambda i: (i, 0)),
            pl.BlockSpec((bm, MLA_KV_LORA), lambda i: (i, 0)),
            pl.BlockSpec((bm, 2 * MLA_ROPE_DIM), lambda i: (i, 0)),
        ],
        out_shape=[
            jax.ShapeDtypeStruct((T, MLA_Q_LORA), BF16),
            jax.ShapeDtypeStruct((T, MLA_KV_LORA), BF16),
            jax.ShapeDtypeStruct((T, 2 * MLA_ROPE_DIM), F32),
        ],
        compiler_params=_params("parallel"),
        name="proj_latents",
    )(x2d, gain, w, gq, gkv)


def _retention_kernel(q_ref, k_ref, v_ref, g_ref, dec_ref, qd_ref, kd_ref, cd_ref, gain_ref, o_ref, state_ref, *, tb):
    @pl.when(pl.program_id(2) == 0)
    def _():
        state_ref[...] = jnp.zeros_like(state_ref)

    dec = dec_ref[...]
    qd = qd_ref[...]
    kd = kd_ref[...]
    cd = cd_ref[0:1, 0:1]
    gain = gain_ref[...]
    C = RET_CHUNK

    def body(c, carry):
        r0 = pl.multiple_of(c * C, C)
        q = q_ref[pl.ds(r0, C), :]
        k = k_ref[pl.ds(r0, C), :]
        v = v_ref[pl.ds(r0, C), :]
        st = state_ref[...]
        p = (_dot_nt(q, k) * dec).astype(BF16)
        o = _dot(p, v) + _dot(q, st.astype(BF16)) * qd
        kk = (k.astype(F32) * kd).astype(BF16)
        state_ref[...] = st * cd + _dot_tn(kk, v)
        on = _rms(o, gain)
        o_ref[pl.ds(r0, C), :] = (on * g_ref[pl.ds(r0, C), :].astype(F32)).astype(BF16)
        return carry

    lax.fori_loop(0, tb // C, body, 0)


def _retention(qk, vg, ret_gain, B, S, tb=1024):
    T = qk.shape[0]
    H, C, dk, dv = RET_HEADS, RET_CHUNK, RET_QK_DIM, RET_V_DIM
    nt = S // tb
    log_g = jnp.log1p(-jnp.exp2(-5.0 - jnp.arange(H, dtype=F32)))
    idx = jnp.arange(C, dtype=F32)
    diff = idx[:, None] - idx[None, :]
    decay_in = jnp.where(diff >= 0, jnp.exp(jnp.maximum(diff, 0.0)[None] * log_g[:, None, None]), 0.0)
    q_dec = jnp.exp((idx + 1.0)[None, :] * log_g[:, None])
    k_dec = jnp.exp((C - 1.0 - idx)[None, :] * log_g[:, None])
    chunk_dec = jnp.exp(C * log_g)
    qd = jnp.broadcast_to(q_dec[:, :, None], (H, C, dv))
    kd = jnp.broadcast_to(k_dec[:, :, None], (H, C, dk))
    cd = jnp.broadcast_to(chunk_dec[:, None, None], (H, 8, LANES))

    def tab(d1, d2):
        return pl.BlockSpec((None, d1, d2), lambda b, h, t: (h, 0, 0))

    return pl.pallas_call(
        functools.partial(_retention_kernel, tb=tb),
        grid=(B, H, nt),
        in_specs=[
            pl.BlockSpec((tb, dk), lambda b, h, t: (b * nt + t, h)),
            pl.BlockSpec((tb, dk), lambda b, h, t: (b * nt + t, H + h)),
            pl.BlockSpec((tb, dv), lambda b, h, t: (b * nt + t, h)),
            pl.BlockSpec((tb, dv), lambda b, h, t: (b * nt + t, H + h)),
            tab(C, C), tab(C, dv), tab(C, dk), tab(8, LANES),
            pl.BlockSpec((1, dv), lambda b, h, t: (0, h)),
        ],
        out_specs=pl.BlockSpec((tb, dv), lambda b, h, t: (b * nt + t, h)),
        out_shape=jax.ShapeDtypeStruct((T, RET_V_W), BF16),
        scratch_shapes=[pltpu.VMEM((dk, dv), F32)],
        compiler_params=_params("parallel", "parallel", "arbitrary"),
        name="retention",
    )(qk, qk, vg, vg, decay_in, qd, kd, cd, ret_gain)


LOG2_E = 1.4426950408889634


def _mla_qkv_kernel(cq_ref, ckv_ref, kr_ref, cm_ref, sm_ref, wq_ref, wkn_ref, wvt_ref,
                    gq_ref, gkn_ref, gkr_ref, q_ref, k_ref, vt_ref):
    ckv = ckv_ref[...]
    yq = _dot(cq_ref[...], wq_ref[...])
    ykn = _dot(ckv, wkn_ref[...])
    vt_ref[...] = _dot_nt(wvt_ref[...], ckv).astype(BF16)
    cm = cm_ref[...]
    sm = sm_ref[...]
    first = (lax.broadcasted_iota(jnp.int32, (1, LANES), 1) < MLA_ROPE_DIM).astype(F32)
    inv_d = 1.0 / MLA_QK_DIM
    half = MLA_PAD_DIM // 2

    kr = kr_ref[...]
    ss_kr = jnp.sum(jnp.square(kr * first), axis=-1, keepdims=True)
    krg = kr * gkr_ref[...]
    k_roped = krg * cm + pltpu.roll(krg, MLA_ROPE_DIM, 1) * sm

    gq1 = gq_ref[:, :half]
    gq2 = gq_ref[:, half:]
    gkn = gkn_ref[...]
    for h in range(MLA_HEADS):
        lo = h * MLA_PAD_DIM
        y1 = yq[:, lo:lo + half]
        y2 = yq[:, lo + half:lo + 2 * half]
        ss = jnp.sum(y1 * y1, axis=-1, keepdims=True) + jnp.sum(jnp.square(y2 * first), axis=-1, keepdims=True)
        sc = lax.rsqrt(ss * inv_d + RMS_EPS) * (MLA_QK_DIM ** -0.5 * LOG2_E)
        y2g = y2 * gq2 * sc
        q_ref[:, lo:lo + half] = (y1 * gq1 * sc).astype(BF16)
        q_ref[:, lo + half:lo + 2 * half] = (y2g * cm + pltpu.roll(y2g, MLA_ROPE_DIM, 1) * sm).astype(BF16)
        kn = ykn[:, h * half:(h + 1) * half]
        ssk = jnp.sum(kn * kn, axis=-1, keepdims=True) + ss_kr
        sck = lax.rsqrt(ssk * inv_d + RMS_EPS)
        k_ref[:, lo:lo + half] = (kn * gkn * sck).astype(BF16)
        k_ref[:, lo + half:lo + 2 * half] = (k_roped * sck).astype(BF16)


def _mla_qkv(cq, ckv, kr, cm, sm, wq, wkn, wvt, gq, gkn, gkr, bm=512):
    T = cq.shape[0]

    def row(w):
        return pl.BlockSpec((bm, w), lambda i: (i, 0))

    return pl.pallas_call(
        _mla_qkv_kernel,
        grid=(T // bm,),
        in_specs=[row(MLA_Q_LORA), row(MLA_KV_LORA), row(LANES), row(LANES), row(LANES),
                  _full(wq.shape), _full(wkn.shape), _full(wvt.shape),
                  _full(gq.shape), _full(gkn.shape), _full(gkr.shape)],
        out_specs=[row(MLA_QK_W), row(MLA_QK_W), pl.BlockSpec((MLA_V_W, bm), lambda i: (0, i))],
        out_shape=[jax.ShapeDtypeStruct((T, MLA_QK_W), BF16),
                   jax.ShapeDtypeStruct((T, MLA_QK_W), BF16),
                   jax.ShapeDtypeStruct((MLA_V_W, T), BF16)],
        compiler_params=_params("parallel"),
        name="mla_qkv",
    )(cq, ckv, kr, cm, sm, wq, wkn, wvt, gq, gkn, gkr)


ATTN_HEADS_PER_STEP = 2


def _attn_kernel(q_ref, k_ref, vt_ref, o_ref, m_ref, l_ref, acc_ref, *, tq):
    qi = pl.program_id(2)
    m_ref[...] = jnp.full_like(m_ref, -1e30)
    l_ref[...] = jnp.zeros_like(l_ref)
    acc_ref[...] = jnp.zeros_like(acc_ref)

    def block(kj, masked):
        r0 = pl.multiple_of(kj * tq, tq)
        for c in range(ATTN_HEADS_PER_STEP):
            qc = q_ref[:, c * MLA_PAD_DIM:(c + 1) * MLA_PAD_DIM]
            kc = k_ref[pl.ds(r0, tq), c * MLA_PAD_DIM:(c + 1) * MLA_PAD_DIM]
            st = _dot_nt(kc, qc)
            if masked:
                kpos = lax.broadcasted_iota(jnp.int32, (tq, tq), 0)
                qpos = lax.broadcasted_iota(jnp.int32, (tq, tq), 1)
                st = jnp.where(kpos <= qpos, st, -1e30)
            rows = slice(c * MLA_V_DIM, (c + 1) * MLA_V_DIM)
            m_prev = m_ref[c:c + 1, :]
            m_new = jnp.maximum(m_prev, jnp.max(st, axis=0, keepdims=True))
            alpha = jnp.exp2(m_prev - m_new)
            pt = jnp.exp2(st - m_new)
            l_ref[c:c + 1, :] = alpha * l_ref[c:c + 1, :] + jnp.sum(pt, axis=0, keepdims=True)
            acc_ref[rows, :] = alpha * acc_ref[rows, :] + _dot(vt_ref[rows, pl.ds(r0, tq)], pt.astype(BF16))
            m_ref[c:c + 1, :] = m_new

    def body(kj, carry):
        block(kj, False)
        return carry

    lax.fori_loop(0, qi, body, 0)
    block(qi, True)
    for c in range(ATTN_HEADS_PER_STEP):
        rows = slice(c * MLA_V_DIM, (c + 1) * MLA_V_DIM)
        o_ref[rows, :] = (acc_ref[rows, :] / l_ref[c:c + 1, :]).astype(BF16)


def _attention(q, k, vt, B, S, tq=512):
    T = q.shape[0]
    nq = S // tq
    hp = ATTN_HEADS_PER_STEP
    return pl.pallas_call(
        functools.partial(_attn_kernel, tq=tq),
        grid=(B, MLA_HEADS // hp, nq),
        in_specs=[
            pl.BlockSpec((tq, hp * MLA_PAD_DIM), lambda b, h, i: (b * nq + i, h)),
            pl.BlockSpec((S, hp * MLA_PAD_DIM), lambda b, h, i: (b, h)),
            pl.BlockSpec((hp * MLA_V_DIM, S), lambda b, h, i: (h, b)),
        ],
        out_specs=pl.BlockSpec((hp * MLA_V_DIM, tq), lambda b, h, i: (h, b * nq + i)),
        out_shape=jax.ShapeDtypeStruct((MLA_V_W, T), BF16),
        scratch_shapes=[pltpu.VMEM((hp, tq), F32), pltpu.VMEM((hp, tq), F32),
                        pltpu.VMEM((hp * MLA_V_DIM, tq), F32)],
        compiler_params=_params("parallel", "parallel", "arbitrary"),
        name="mla_attention",
    )(q, k, vt)


def _merge_kernel(x_ref, ro_ref, mot_ref, gr_ref, gm_ref, wbr_ref, wbm_ref, wo_ref, o_ref):
    a_ret = _dot(ro_ref[...], wbr_ref[...])
    a_mla = _dot_tn(mot_ref[...], wbm_ref[...])
    mixed = gr_ref[...].astype(F32) * a_ret + gm_ref[...].astype(F32) * a_mla
    o_ref[...] = x_ref[...] + _dot(mixed.astype(BF16), wo_ref[...])


def _merge(x2d, ro, mo, vg, wbr, wbm, wo, bm=512):
    T = x2d.shape[0]
    gate0 = (2 * RET_V_W) // D_MODEL
    return pl.pallas_call(
        _merge_kernel,
        grid=(T // bm,),
        in_specs=[
            pl.BlockSpec((bm, D_MODEL), lambda i: (i, 0)),
            pl.BlockSpec((bm, RET_V_W), lambda i: (i, 0)),
            pl.BlockSpec((MLA_V_W, bm), lambda i: (0, i)),
            pl.BlockSpec((bm, D_MODEL), lambda i: (i, gate0)),
            pl.BlockSpec((bm, D_MODEL), lambda i: (i, gate0 + 1)),
            _full(wbr.shape), _full(wbm.shape), _full(wo.shape),
        ],
        out_specs=pl.BlockSpec((bm, D_MODEL), lambda i: (i, 0)),
        out_shape=jax.ShapeDtypeStruct((T, D_MODEL), F32),
        compiler_params=_params("parallel"),
        name="merge_out_proj",
    )(x2d, ro, mo, vg, vg, wbr, wbm, wo)


def _mlp_kernel(x_ref, gain_ref, wup_ref, wdown_ref, o_ref, *, bf):
    x = x_ref[...]
    h = _rms(x, gain_ref[...]).astype(BF16)
    acc = x
    for c in range(D_FF // bf):
        up = _dot(h, wup_ref[:, c * bf:(c + 1) * bf])
        a = jnp.square(jnp.maximum(up, 0.0)).astype(BF16)
        acc = acc + _dot(a, wdown_ref[c * bf:(c + 1) * bf, :])
    o_ref[...] = acc


def _mlp(x2d, gain, wup, wdown, bm=512, bf=1024):
    T = x2d.shape[0]
    return pl.pallas_call(
        functools.partial(_mlp_kernel, bf=bf),
        grid=(T // bm,),
        in_specs=[pl.BlockSpec((bm, D_MODEL), lambda i: (i, 0)), _full((1, D_MODEL)),
                  _full(wup.shape), _full(wdown.shape)],
        out_specs=pl.BlockSpec((bm, D_MODEL), lambda i: (i, 0)),
        out_shape=jax.ShapeDtypeStruct((T, D_MODEL), F32),
        compiler_params=_params("parallel"),
        name="mlp",
    )(x2d, gain, wup, wdown)


def _ple_kernel(x_ref, p_ref, gain_ref, wg_ref, wp_ref, o_ref):
    x = x_ref[...]
    h = _rms(x, gain_ref[...]).astype(BF16)
    gate = jax.nn.sigmoid(_dot(h, wg_ref[...]))
    o_ref[...] = x + gate * _dot(p_ref[...].astype(BF16), wp_ref[...])


def _ple(x2d, p2d, gain, wg, wp, bm=1024):
    T = x2d.shape[0]
    return pl.pallas_call(
        _ple_kernel,
        grid=(T // bm,),
        in_specs=[pl.BlockSpec((bm, D_MODEL), lambda i: (i, 0)), pl.BlockSpec((bm, PLE_DIM), lambda i: (i, 0)),
                  _full((1, D_MODEL)), _full(wg.shape), _full(wp.shape)],
        out_specs=pl.BlockSpec((bm, D_MODEL), lambda i: (i, 0)),
        out_shape=jax.ShapeDtypeStruct((T, D_MODEL), F32),
        compiler_params=_params("parallel"),
        name="ple",
    )(x2d, p2d, gain, wg, wp)


def _rotate_half_cols(w):
    half = w.shape[-1] // 2
    return jnp.concatenate([-w[..., half:], w[..., :half]], axis=-1)


def _swap_halves(g):
    half = g.shape[-1] // 2
    return jnp.concatenate([g[..., half:], g[..., :half]], axis=-1)


def _layer(x2d, p2d, tables, B, S, norm_mix, w_in, ret_norm, q_lat_norm, kv_lat_norm, w_uq, w_ukv,
           q_norm, k_norm, w_br, w_bm, w_o, norm_mlp, w_up, w_down, norm_ple, w_ple_gate, w_ple):
    cos_r, sin_r, cos_m, sin_m = tables
    o_rv = 2 * RET_QK_W
    o_cq = o_rv + 2 * RET_V_W
    o_ckv = o_cq + MLA_Q_LORA
    o_kr = o_ckv + MLA_KV_LORA
    o_gr = o_kr + MLA_ROPE_DIM

    gain_mix = norm_mix[None, :]
    w_qk = w_in[:, :o_rv].astype(BF16)
    w_vg = jnp.concatenate([w_in[:, o_rv:o_cq], w_in[:, o_gr:]], axis=1).astype(BF16)
    w_kr = w_in[:, o_kr:o_gr]
    w_lat = jnp.concatenate([w_in[:, o_cq:o_gr], _rotate_half_cols(w_kr)], axis=1).astype(BF16)

    qk = _proj_qk(x2d, gain_mix, w_qk, cos_r, sin_r)
    vg = _proj_act(x2d, gain_mix, w_vg)
    cq, ckv, kr = _proj_lat(x2d, gain_mix, w_lat, q_lat_norm[None, :], kv_lat_norm[None, :])

    ro = _retention(qk, vg, ret_norm[None, :], B, S)

    wq = w_uq.reshape(MLA_Q_LORA, MLA_HEADS, MLA_QK_DIM)
    wq_rope = wq[:, :, MLA_NOPE_DIM:]
    wq = jnp.concatenate([wq, _rotate_half_cols(wq_rope)], axis=-1).reshape(MLA_Q_LORA, MLA_QK_W).astype(BF16)
    wkv = w_ukv.reshape(MLA_KV_LORA, MLA_HEADS, MLA_NOPE_DIM + MLA_V_DIM)
    wkn = wkv[:, :, :MLA_NOPE_DIM].reshape(MLA_KV_LORA, MLA_HEADS * MLA_NOPE_DIM).astype(BF16)
    wvt = wkv[:, :, MLA_NOPE_DIM:].reshape(MLA_KV_LORA, MLA_V_W).T.astype(BF16)
    gq = jnp.concatenate([q_norm, _swap_halves(q_norm[MLA_NOPE_DIM:])])[None, :]
    gkn = k_norm[None, :MLA_NOPE_DIM]
    gkr = jnp.concatenate([k_norm[MLA_NOPE_DIM:], _swap_halves(k_norm[MLA_NOPE_DIM:])])[None, :]
    q, k, vt = _mla_qkv(cq, ckv, kr, cos_m, sin_m, wq, wkn, wvt, gq, gkn, gkr)
    mo = _attention(q, k, vt, B, S)

    x2d = _merge(x2d, ro, mo, vg, w_br.astype(BF16), w_bm.astype(BF16), w_o.astype(BF16))
    x2d = _mlp(x2d, norm_mlp[None, :], w_up.astype(BF16), w_down.astype(BF16))
    x2d = _ple(x2d, p2d, norm_ple[None, :], w_ple_gate.astype(BF16), w_ple.astype(BF16))
    return x2d


def kernel(x, p, positions, norm_mix, w_in, ret_norm, q_lat_norm, kv_lat_norm, w_uq, w_ukv, q_norm, k_norm,
           w_br, w_bm, w_o, norm_mlp, w_up, w_down, norm_ple, w_ple_gate, w_ple):
    B, S, D = x.shape
    T = B * S
    x2d = x.reshape(T, D)
    pos_f = positions.astype(F32).reshape(T, 1)
    tables = _rope_tables(pos_f)
    for i in range(p.shape[0]):
        x2d = _layer(x2d, p[i].reshape(T, PLE_DIM), tables, B, S, norm_mix[i], w_in[i], ret_norm[i],
                     q_lat_norm[i], kv_lat_norm[i], w_uq[i], w_ukv[i], q_norm[i], k_norm[i], w_br[i], w_bm[i],
                     w_o[i], norm_mlp[i], w_up[i], w_down[i], norm_ple[i], w_ple_gate[i], w_ple[i])
    return x2d.reshape(B, S, D)
```

```python
import functools

import jax
import jax.numpy as jnp
from jax import lax
from jax.experimental import pallas as pl
from jax.experimental.pallas import tpu as pltpu

D_MODEL = 1024
PLE_DIM = 256
RET_HEADS = 4
RET_QK_DIM = 256
RET_V_DIM = 512
RET_CHUNK = 128
MLA_HEADS = 8
MLA_NOPE_DIM = 128
MLA_ROPE_DIM = 64
MLA_QK_DIM = MLA_NOPE_DIM + MLA_ROPE_DIM
MLA_V_DIM = 128
MLA_Q_LORA = 384
MLA_KV_LORA = 256
D_FF = 4 * D_MODEL
ROPE_BASE = 10000.0
RMS_EPS = 1e-6

RET_QK_W = RET_HEADS * RET_QK_DIM
RET_V_W = RET_HEADS * RET_V_DIM
MLA_V_W = MLA_HEADS * MLA_V_DIM
MLA_PAD_DIM = 256
MLA_QK_W = MLA_HEADS * MLA_PAD_DIM

LANES = 128
VMEM_LIMIT = 56 * 1024 * 1024

F32 = jnp.float32
BF16 = jnp.bfloat16


def _params(*sem):
    return pltpu.CompilerParams(dimension_semantics=sem, vmem_limit_bytes=VMEM_LIMIT)


def _full(shape):
    nd = len(shape)
    return pl.BlockSpec(shape, lambda *_: (0,) * nd)


def _rms(x, gain):
    ms = jnp.mean(x * x, axis=-1, keepdims=True)
    return x * lax.rsqrt(ms + RMS_EPS) * gain


def _dot(a, b):
    return jnp.dot(a, b, preferred_element_type=F32)


def _dot_nt(a, b):
    return lax.dot_general(a, b, (((1,), (1,)), ((), ())), preferred_element_type=F32)


def _dot_tn(a, b):
    return lax.dot_general(a, b, (((0,), (0,)), ((), ())), preferred_element_type=F32)


def _rope_tables_kernel(pos_ref, inv_r_ref, inv_m_ref, mask_ref, cr_ref, sr_ref, cm_ref, sm_ref):
    pos = pos_ref[...]
    ang = pos * inv_r_ref[...]
    cr_ref[...] = jnp.cos(ang)
    sr_ref[...] = jnp.sin(ang)
    ang_m = pos * inv_m_ref[...]
    mask = mask_ref[...]
    cm_ref[...] = jnp.cos(ang_m) * mask
    sm_ref[...] = jnp.sin(ang_m) * mask


def _rope_tables(pos_f, bm=2048):
    T = pos_f.shape[0]
    half_r = RET_QK_DIM // 2
    half_m = MLA_ROPE_DIM // 2
    inv_r = ROPE_BASE ** (-jnp.arange(half_r, dtype=F32) / half_r)
    inv_m = ROPE_BASE ** (-jnp.arange(half_m, dtype=F32) / half_m)
    zeros = jnp.zeros((LANES - 2 * half_m,), F32)
    inv_m2 = jnp.concatenate([inv_m, inv_m, zeros])[None, :]
    mask = jnp.concatenate([jnp.ones((2 * half_m,), F32), zeros])[None, :]
    row = pl.BlockSpec((bm, LANES), lambda i: (i, 0))
    out = jax.ShapeDtypeStruct((T, LANES), F32)
    return pl.pallas_call(
        _rope_tables_kernel,
        grid=(T // bm,),
        in_specs=[pl.BlockSpec((bm, 1), lambda i: (i, 0)), _full((1, LANES)), _full((1, LANES)), _full((1, LANES))],
        out_specs=[row, row, row, row],
        out_shape=[out, out, out, out],
        compiler_params=_params("parallel"),
        name="rope_tables",
    )(pos_f, inv_r[None, :], inv_m2, mask)


def _proj_qk_kernel(x_ref, gain_ref, w_ref, cos_ref, sin_ref, o_ref, h_ref):
    j = pl.program_id(1)

    @pl.when(j == 0)
    def _():
        h_ref[...] = _rms(x_ref[...], gain_ref[...]).astype(BF16)

    acc = _dot(h_ref[...], w_ref[...])
    scale = jnp.where(j == 1, RET_QK_DIM ** -0.5, 1.0).astype(F32)
    cos = cos_ref[...]
    sin = sin_ref[...]
    half = RET_QK_DIM // 2
    for hd in range(RET_HEADS):
        lo = hd * RET_QK_DIM
        x1 = acc[:, lo:lo + half]
        x2 = acc[:, lo + half:lo + 2 * half]
        o_ref[:, lo:lo + half] = ((x1 * cos - x2 * sin) * scale).astype(BF16)
        o_ref[:, lo + half:lo + 2 * half] = ((x2 * cos + x1 * sin) * scale).astype(BF16)


def _proj_qk(x2d, gain, w, cos_r, sin_r, bm=1024):
    T = x2d.shape[0]
    bn = RET_QK_W
    return pl.pallas_call(
        _proj_qk_kernel,
        grid=(T // bm, 2),
        in_specs=[
            pl.BlockSpec((bm, D_MODEL), lambda i, j: (i, 0)),
            _full((1, D_MODEL)),
            pl.BlockSpec((D_MODEL, bn), lambda i, j: (0, j)),
            pl.BlockSpec((bm, LANES), lambda i, j: (i, 0)),
            pl.BlockSpec((bm, LANES), lambda i, j: (i, 0)),
        ],
        out_specs=pl.BlockSpec((bm, bn), lambda i, j: (i, j)),
        out_shape=jax.ShapeDtypeStruct((T, 2 * bn), BF16),
        scratch_shapes=[pltpu.VMEM((bm, D_MODEL), BF16)],
        compiler_params=_params("parallel", "arbitrary"),
        name="proj_ret_qk",
    )(x2d, gain, w, cos_r, sin_r)


def _proj_act_kernel(x_ref, gain_ref, w_ref, o_ref, h_ref):
    j = pl.program_id(1)

    @pl.when(j == 0)
    def _():
        h_ref[...] = _rms(x_ref[...], gain_ref[...]).astype(BF16)

    acc = _dot(h_ref[...], w_ref[...])

    @pl.when(j < 2)
    def _():
        o_ref[...] = acc.astype(BF16)

    @pl.when(jnp.logical_and(j >= 2, j < 4))
    def _():
        o_ref[...] = (acc * jax.nn.sigmoid(acc)).astype(BF16)

    @pl.when(j >= 4)
    def _():
        o_ref[...] = jax.nn.sigmoid(acc).astype(BF16)


def _proj_act(x2d, gain, w, bm=1024, bn=1024):
    T = x2d.shape[0]
    n = w.shape[1]
    return pl.pallas_call(
        _proj_act_kernel,
        grid=(T // bm, n // bn),
        in_specs=[
            pl.BlockSpec((bm, D_MODEL), lambda i, j: (i, 0)),
            _full((1, D_MODEL)),
            pl.BlockSpec((D_MODEL, bn), lambda i, j: (0, j)),
        ],
        out_specs=pl.BlockSpec((bm, bn), lambda i, j: (i, j)),
        out_shape=jax.ShapeDtypeStruct((T, n), BF16),
        scratch_shapes=[pltpu.VMEM((bm, D_MODEL), BF16)],
        compiler_params=_params("parallel", "arbitrary"),
        name="proj_v_gates",
    )(x2d, gain, w)


def _proj_lat_kernel(x_ref, gain_ref, w_ref, gq_ref, gkv_ref, cq_ref, ckv_ref, kr_ref):
    h = _rms(x_ref[...], gain_ref[...]).astype(BF16)
    acc = _dot(h, w_ref[...])
    cq_ref[...] = _rms(acc[:, :MLA_Q_LORA], gq_ref[...]).astype(BF16)
    ckv_ref[...] = _rms(acc[:, MLA_Q_LORA:MLA_Q_LORA + MLA_KV_LORA], gkv_ref[...]).astype(BF16)
    kr_ref[...] = acc[:, MLA_Q_LORA + MLA_KV_LORA:]


def _proj_lat(x2d, gain, w, gq, gkv, bm=1024):
    T = x2d.shape[0]
    n = w.shape[1]
    return pl.pallas_call(
        _proj_lat_kernel,
        grid=(T // bm,),
        in_specs=[
            pl.BlockSpec((bm, D_MODEL), lambda i: (i, 0)),
            _full((1, D_MODEL)),
            _full((D_MODEL, n)),
            _full((1, MLA_Q_LORA)),
            _full((1, MLA_KV_LORA)),
        ],
        out_specs=[
            pl.BlockSpec((bm, MLA_Q_LORA), lambda i: (i, 0)),
            pl.BlockSpec((bm, MLA_KV_LORA), lambda i: (i, 0)),
            pl.BlockSpec((bm, 2 * MLA_ROPE_DIM), lambda i: (i, 0)),
        ],
        out_shape=[
            jax.ShapeDtypeStruct((T, MLA_Q_LORA), BF16),
            jax.ShapeDtypeStruct((T, MLA_KV_LORA), BF16),
            jax.ShapeDtypeStruct((T, 2 * MLA_ROPE_DIM), F32),
        ],
        compiler_params=_params("parallel"),
        name="proj_latents",
    )(x2d, gain, w, gq, gkv)


def _retention_kernel(q_ref, k_ref, v_ref, g_ref, dec_ref, qd_ref, kd_ref, cd_ref, gain_ref, o_ref, state_ref, *, tb):
    @pl.when(pl.program_id(2) == 0)
    def _():
        state_ref[...] = jnp.zeros_like(state_ref)

    dec = dec_ref[...]
    qd = qd_ref[...]
    kd = kd_ref[...]
    cd = cd_ref[0:1, 0:1]
    gain = gain_ref[...]
    C = RET_CHUNK

    def body(c, carry):
        r0 = pl.multiple_of(c * C, C)
        q = q_ref[pl.ds(r0, C), :]
        k = k_ref[pl.ds(r0, C), :]
        v = v_ref[pl.ds(r0, C), :]
        st = state_ref[...]
        p = (_dot_nt(q, k) * dec).astype(BF16)
        o = _dot(p, v) + _dot(q, st.astype(BF16)) * qd
        kk = (k.astype(F32) * kd).astype(BF16)
        state_ref[...] = st * cd + _dot_tn(kk, v)
        on = _rms(o, gain)
        o_ref[pl.ds(r0, C), :] = (on * g_ref[pl.ds(r0, C), :].astype(F32)).astype(BF16)
        return carry

    lax.fori_loop(0, tb // C, body, 0)


def _retention(qk, vg, ret_gain, B, S, tb=1024):
    T = qk.shape[0]
    H, C, dk, dv = RET_HEADS, RET_CHUNK, RET_QK_DIM, RET_V_DIM
    nt = S // tb
    log_g = jnp.log1p(-jnp.exp2(-5.0 - jnp.arange(H, dtype=F32)))
    idx = jnp.arange(C, dtype=F32)
    diff = idx[:, None] - idx[None, :]
    decay_in = jnp.where(diff >= 0, jnp.exp(jnp.maximum(diff, 0.0)[None] * log_g[:, None, None]), 0.0)
    q_dec = jnp.exp((idx + 1.0)[None, :] * log_g[:, None])
    k_dec = jnp.exp((C - 1.0 - idx)[None, :] * log_g[:, None])
    chunk_dec = jnp.exp(C * log_g)
    qd = jnp.broadcast_to(q_dec[:, :, None], (H, C, dv))
    kd = jnp.broadcast_to(k_dec[:, :, None], (H, C, dk))
    cd = jnp.broadcast_to(chunk_dec[:, None, None], (H, 8, LANES))

    def tab(d1, d2):
        return pl.BlockSpec((None, d1, d2), lambda b, h, t: (h, 0, 0))

    return pl.pallas_call(
        functools.partial(_retention_kernel, tb=tb),
        grid=(B, H, nt),
        in_specs=[
            pl.BlockSpec((tb, dk), lambda b, h, t: (b * nt + t, h)),
            pl.BlockSpec((tb, dk), lambda b, h, t: (b * nt + t, H + h)),
            pl.BlockSpec((tb, dv), lambda b, h, t: (b * nt + t, h)),
            pl.BlockSpec((tb, dv), lambda b, h, t: (b * nt + t, H + h)),
            tab(C, C), tab(C, dv), tab(C, dk), tab(8, LANES),
            pl.BlockSpec((1, dv), lambda b, h, t: (0, h)),
        ],
        out_specs=pl.BlockSpec((tb, dv), lambda b, h, t: (b * nt + t, h)),
        out_shape=jax.ShapeDtypeStruct((T, RET_V_W), BF16),
        scratch_shapes=[pltpu.VMEM((dk, dv), F32)],
        compiler_params=_params("parallel", "parallel", "arbitrary"),
        name="retention",
    )(qk, qk, vg, vg, decay_in, qd, kd, cd, ret_gain)


LOG2_E = 1.4426950408889634


def _mla_qkv_kernel(cq_ref, ckv_ref, kr_ref, cm_ref, sm_ref, wq_ref, wkn_ref, wvt_ref,
                    gq_ref, gkn_ref, gkr_ref, q_ref, k_ref, vt_ref):
    ckv = ckv_ref[...]
    yq = _dot(cq_ref[...], wq_ref[...])
    ykn = _dot(ckv, wkn_ref[...])
    vt_ref[...] = _dot_nt(wvt_ref[...], ckv).astype(BF16)
    cm = cm_ref[...]
    sm = sm_ref[...]
    first = (lax.broadcasted_iota(jnp.int32, (1, LANES), 1) < MLA_ROPE_DIM).astype(F32)
    inv_d = 1.0 / MLA_QK_DIM
    half = MLA_PAD_DIM // 2

    kr = kr_ref[...]
    ss_kr = jnp.sum(jnp.square(kr * first), axis=-1, keepdims=True)
    krg = kr * gkr_ref[...]
    k_roped = krg * cm + pltpu.roll(krg, MLA_ROPE_DIM, 1) * sm

    gq1 = gq_ref[:, :half]
    gq2 = gq_ref[:, half:]
    gkn = gkn_ref[...]
    for h in range(MLA_HEADS):
        lo = h * MLA_PAD_DIM
        y1 = yq[:, lo:lo + half]
        y2 = yq[:, lo + half:lo + 2 * half]
        ss = jnp.sum(y1 * y1, axis=-1, keepdims=True) + jnp.sum(jnp.square(y2 * first), axis=-1, keepdims=True)
        sc = lax.rsqrt(ss * inv_d + RMS_EPS) * (MLA_QK_DIM ** -0.5 * LOG2_E)
        y2g = y2 * gq2 * sc
        q_ref[:, lo:lo + half] = (y1 * gq1 * sc).astype(BF16)
        q_ref[:, lo + half:lo + 2 * half] = (y2g * cm + pltpu.roll(y2g, MLA_ROPE_DIM, 1) * sm).astype(BF16)
        kn = ykn[:, h * half:(h + 1) * half]
        ssk = jnp.sum(kn * kn, axis=-1, keepdims=True) + ss_kr
        sck = lax.rsqrt(ssk * inv_d + RMS_EPS)
        k_ref[:, lo:lo + half] = (kn * gkn * sck).astype(BF16)
        k_ref[:, lo + half:lo + 2 * half] = (k_roped * sck).astype(BF16)


def _mla_qkv(cq, ckv, kr, cm, sm, wq, wkn, wvt, gq, gkn, gkr, bm=512):
    T = cq.shape[0]

    def row(w):
        return pl.BlockSpec((bm, w), lambda i: (i, 0))

    return pl.pallas_call(
        _mla_qkv_kernel,
        grid=(T // bm,),
        in_specs=[row(MLA_Q_LORA), row(MLA_KV_LORA), row(LANES), row(LANES), row(LANES),
                  _full(wq.shape), _full(wkn.shape), _full(wvt.shape),
                  _full(gq.shape), _full(gkn.shape), _full(gkr.shape)],
        out_specs=[row(MLA_QK_W), row(MLA_QK_W), pl.BlockSpec((MLA_V_W, bm), lambda i: (0, i))],
        out_shape=[jax.ShapeDtypeStruct((T, MLA_QK_W), BF16),
                   jax.ShapeDtypeStruct((T, MLA_QK_W), BF16),
                   jax.ShapeDtypeStruct((MLA_V_W, T), BF16)],
        compiler_params=_params("parallel"),
        name="mla_qkv",
    )(cq, ckv, kr, cm, sm, wq, wkn, wvt, gq, gkn, gkr)


ATTN_HEADS_PER_STEP = 2


def _attn_kernel(q_ref, k_ref, vt_ref, o_ref, m_ref, l_ref, acc_ref, s_ref, bm_ref, *, tq):
    qi = pl.program_id(2)
    m_ref[...] = jnp.full_like(m_ref, -1e30)
    l_ref[...] = jnp.zeros_like(l_ref)
    acc_ref[...] = jnp.zeros_like(acc_ref)

    def scores(c, kj):
        r0 = pl.multiple_of(kj * tq, tq)
        cols = slice(c * MLA_PAD_DIM, (c + 1) * MLA_PAD_DIM)
        return _dot_nt(k_ref[pl.ds(r0, tq), cols], q_ref[:, cols])

    def park(kj):
        st = scores(0, kj)
        s_ref[...] = st
        bm_ref[...] = jnp.max(st, axis=0, keepdims=True)

    def update(c, kj, st, masked, block_max=None):
        r0 = pl.multiple_of(kj * tq, tq)
        if masked:
            kpos = lax.broadcasted_iota(jnp.int32, (tq, tq), 0)
            qpos = lax.broadcasted_iota(jnp.int32, (tq, tq), 1)
            st = jnp.where(kpos <= qpos, st, -1e30)
        if block_max is None:
            block_max = jnp.max(st, axis=0, keepdims=True)
        rows = slice(c * MLA_V_DIM, (c + 1) * MLA_V_DIM)
        m_prev = m_ref[c:c + 1, :]
        m_new = jnp.maximum(m_prev, block_max)
        alpha = jnp.exp2(m_prev - m_new)
        pt = jnp.exp2(st - m_new)
        l_ref[c:c + 1, :] = alpha * l_ref[c:c + 1, :] + jnp.sum(pt, axis=0, keepdims=True)
        acc_ref[rows, :] = alpha * acc_ref[rows, :] + _dot(vt_ref[rows, pl.ds(r0, tq)], pt.astype(BF16))
        m_ref[c:c + 1, :] = m_new

    def step(kj, masked):
        s1 = scores(1, kj)
        update(0, kj, s_ref[...], masked, None if masked else bm_ref[...])
        if not masked:
            park(kj + 1)
        update(1, kj, s1, masked)

    def body(kj, carry):
        step(kj, False)
        return carry

    park(0)
    lax.fori_loop(0, qi, body, 0)
    step(qi, True)
    for c in range(ATTN_HEADS_PER_STEP):
        rows = slice(c * MLA_V_DIM, (c + 1) * MLA_V_DIM)
        o_ref[rows, :] = (acc_ref[rows, :] / l_ref[c:c + 1, :]).astype(BF16)


def _attention(q, k, vt, B, S, tq=512):
    T = q.shape[0]
    nq = S // tq
    hp = ATTN_HEADS_PER_STEP
    return pl.pallas_call(
        functools.partial(_attn_kernel, tq=tq),
        grid=(B, MLA_HEADS // hp, nq),
        in_specs=[
            pl.BlockSpec((tq, hp * MLA_PAD_DIM), lambda b, h, i: (b * nq + i, h)),
            pl.BlockSpec((S, hp * MLA_PAD_DIM), lambda b, h, i: (b, h)),
            pl.BlockSpec((hp * MLA_V_DIM, S), lambda b, h, i: (h, b)),
        ],
        out_specs=pl.BlockSpec((hp * MLA_V_DIM, tq), lambda b, h, i: (h, b * nq + i)),
        out_shape=jax.ShapeDtypeStruct((MLA_V_W, T), BF16),
        scratch_shapes=[pltpu.VMEM((hp, tq), F32), pltpu.VMEM((hp, tq), F32),
                        pltpu.VMEM((hp * MLA_V_DIM, tq), F32), pltpu.VMEM((tq, tq), F32),
                        pltpu.VMEM((1, tq), F32)],
        compiler_params=_params("parallel", "parallel", "arbitrary"),
        name="mla_attention",
    )(q, k, vt)


def _merge_kernel(x_ref, ro_ref, mot_ref, gr_ref, gm_ref, wbr_ref, wbm_ref, wo_ref, o_ref):
    a_ret = _dot(ro_ref[...], wbr_ref[...])
    a_mla = _dot_tn(mot_ref[...], wbm_ref[...])
    mixed = gr_ref[...].astype(F32) * a_ret + gm_ref[...].astype(F32) * a_mla
    o_ref[...] = x_ref[...] + _dot(mixed.astype(BF16), wo_ref[...])


def _merge(x2d, ro, mo, vg, wbr, wbm, wo, bm=512):
    T = x2d.shape[0]
    gate0 = (2 * RET_V_W) // D_MODEL
    return pl.pallas_call(
        _merge_kernel,
        grid=(T // bm,),
        in_specs=[
            pl.BlockSpec((bm, D_MODEL), lambda i: (i, 0)),
            pl.BlockSpec((bm, RET_V_W), lambda i: (i, 0)),
            pl.BlockSpec((MLA_V_W, bm), lambda i: (0, i)),
            pl.BlockSpec((bm, D_MODEL), lambda i: (i, gate0)),
            pl.BlockSpec((bm, D_MODEL), lambda i: (i, gate0 + 1)),
            _full(wbr.shape), _full(wbm.shape), _full(wo.shape),
        ],
        out_specs=pl.BlockSpec((bm, D_MODEL), lambda i: (i, 0)),
        out_shape=jax.ShapeDtypeStruct((T, D_MODEL), F32),
        compiler_params=_params("parallel"),
        name="merge_out_proj",
    )(x2d, ro, mo, vg, vg, wbr, wbm, wo)


def _mlp_kernel(x_ref, gain_ref, wup_ref, wdown_ref, o_ref, *, bf):
    x = x_ref[...]
    h = _rms(x, gain_ref[...]).astype(BF16)
    acc = x
    for c in range(D_FF // bf):
        up = _dot(h, wup_ref[:, c * bf:(c + 1) * bf])
        a = jnp.square(jnp.maximum(up, 0.0)).astype(BF16)
        acc = acc + _dot(a, wdown_ref[c * bf:(c + 1) * bf, :])
    o_ref[...] = acc


def _mlp(x2d, gain, wup, wdown, bm=512, bf=1024):
    T = x2d.shape[0]
    return pl.pallas_call(
        functools.partial(_mlp_kernel, bf=bf),
        grid=(T // bm,),
        in_specs=[pl.BlockSpec((bm, D_MODEL), lambda i: (i, 0)), _full((1, D_MODEL)),
                  _full(wup.shape), _full(wdown.shape)],
        out_specs=pl.BlockSpec((bm, D_MODEL), lambda i: (i, 0)),
        out_shape=jax.ShapeDtypeStruct((T, D_MODEL), F32),
        compiler_params=_params("parallel"),
        name="mlp",
    )(x2d, gain, wup, wdown)


def _ple_kernel(x_ref, p_ref, gain_ref, wg_ref, wp_ref, o_ref):
    x = x_ref[...]
    h = _rms(x, gain_ref[...]).astype(BF16)
    gate = jax.nn.sigmoid(_dot(h, wg_ref[...]))
    o_ref[...] = x + gate * _dot(p_ref[...].astype(BF16), wp_ref[...])


def _ple(x2d, p2d, gain, wg, wp, bm=1024):
    T = x2d.shape[0]
    return pl.pallas_call(
        _ple_kernel,
        grid=(T // bm,),
        in_specs=[pl.BlockSpec((bm, D_MODEL), lambda i: (i, 0)), pl.BlockSpec((bm, PLE_DIM), lambda i: (i, 0)),
                  _full((1, D_MODEL)), _full(wg.shape), _full(wp.shape)],
        out_specs=pl.BlockSpec((bm, D_MODEL), lambda i: (i, 0)),
        out_shape=jax.ShapeDtypeStruct((T, D_MODEL), F32),
        compiler_params=_params("parallel"),
        name="ple",
    )(x2d, p2d, gain, wg, wp)


def _rotate_half_cols(w):
    half = w.shape[-1] // 2
    return jnp.concatenate([-w[..., half:], w[..., :half]], axis=-1)


def _swap_halves(g):
    half = g.shape[-1] // 2
    return jnp.concatenate([g[..., half:], g[..., :half]], axis=-1)


def _layer(x2d, p2d, tables, B, S, norm_mix, w_in, ret_norm, q_lat_norm, kv_lat_norm, w_uq, w_ukv,
           q_norm, k_norm, w_br, w_bm, w_o, norm_mlp, w_up, w_down, norm_ple, w_ple_gate, w_ple):
    cos_r, sin_r, cos_m, sin_m = tables
    o_rv = 2 * RET_QK_W
    o_cq = o_rv + 2 * RET_V_W
    o_ckv = o_cq + MLA_Q_LORA
    o_kr = o_ckv + MLA_KV_LORA
    o_gr = o_kr + MLA_ROPE_DIM

    gain_mix = norm_mix[None, :]
    w_qk = w_in[:, :o_rv].astype(BF16)
    w_vg = jnp.concatenate([w_in[:, o_rv:o_cq], w_in[:, o_gr:]], axis=1).astype(BF16)
    w_kr = w_in[:, o_kr:o_gr]
    w_lat = jnp.concatenate([w_in[:, o_cq:o_gr], _rotate_half_cols(w_kr)], axis=1).astype(BF16)

    qk = _proj_qk(x2d, gain_mix, w_qk, cos_r, sin_r)
    vg = _proj_act(x2d, gain_mix, w_vg)
    cq, ckv, kr = _proj_lat(x2d, gain_mix, w_lat, q_lat_norm[None, :], kv_lat_norm[None, :])

    ro = _retention(qk, vg, ret_norm[None, :], B, S)

    wq = w_uq.reshape(MLA_Q_LORA, MLA_HEADS, MLA_QK_DIM)
    wq_rope = wq[:, :, MLA_NOPE_DIM:]
    wq = jnp.concatenate([wq, _rotate_half_cols(wq_rope)], axis=-1).reshape(MLA_Q_LORA, MLA_QK_W).astype(BF16)
    wkv = w_ukv.reshape(MLA_KV_LORA, MLA_HEADS, MLA_NOPE_DIM + MLA_V_DIM)
    wkn = wkv[:, :, :MLA_NOPE_DIM].reshape(MLA_KV_LORA, MLA_HEADS * MLA_NOPE_DIM).astype(BF16)
    wvt = wkv[:, :, MLA_NOPE_DIM:].reshape(MLA_KV_LORA, MLA_V_W).T.astype(BF16)
    gq = jnp.concatenate([q_norm, _swap_halves(q_norm[MLA_NOPE_DIM:])])[None, :]
    gkn = k_norm[None, :MLA_NOPE_DIM]
    gkr = jnp.concatenate([k_norm[MLA_NOPE_DIM:], _swap_halves(k_norm[MLA_NOPE_DIM:])])[None, :]
    q, k, vt = _mla_qkv(cq, ckv, kr, cos_m, sin_m, wq, wkn, wvt, gq, gkn, gkr)
    mo = _attention(q, k, vt, B, S)

    x2d = _merge(x2d, ro, mo, vg, w_br.astype(BF16), w_bm.astype(BF16), w_o.astype(BF16))
    x2d = _mlp(x2d, norm_mlp[None, :], w_up.astype(BF16), w_down.astype(BF16))
    x2d = _ple(x2d, p2d, norm_ple[None, :], w_ple_gate.astype(BF16), w_ple.astype(BF16))
    return x2d


def kernel(x, p, positions, norm_mix, w_in, ret_norm, q_lat_norm, kv_lat_norm, w_uq, w_ukv, q_norm, k_norm,
           w_br, w_bm, w_o, norm_mlp, w_up, w_down, norm_ple, w_ple_gate, w_ple):
    B, S, D = x.shape
    T = B * S
    x2d = x.reshape(T, D)
    pos_f = positions.astype(F32).reshape(T, 1)
    tables = _rope_tables(pos_f)
    for i in range(p.shape[0]):
        x2d = _layer(x2d, p[i].reshape(T, PLE_DIM), tables, B, S, norm_mix[i], w_in[i], ret_norm[i],
                     q_lat_norm[i], kv_lat_norm[i], w_uq[i], w_ukv[i], q_norm[i], k_norm[i], w_br[i], w_bm[i],
                     w_o[i], norm_mlp[i], w_up[i], w_down[i], norm_ple[i], w_ple_gate[i], w_ple[i])
    return x2d.reshape(B, S, D)
```

```python
import functools

import jax
import jax.numpy as jnp
from jax import lax
from jax.experimental import pallas as pl
from jax.experimental.pallas import tpu as pltpu

D_MODEL = 1024
PLE_DIM = 256
RET_HEADS = 4
RET_QK_DIM = 256
RET_V_DIM = 512
RET_CHUNK = 128
MLA_HEADS = 8
MLA_NOPE_DIM = 128
MLA_ROPE_DIM = 64
MLA_QK_DIM = MLA_NOPE_DIM + MLA_ROPE_DIM
MLA_V_DIM = 128
MLA_Q_LORA = 384
MLA_KV_LORA = 256
D_FF = 4 * D_MODEL
ROPE_BASE = 10000.0
RMS_EPS = 1e-6

RET_QK_W = RET_HEADS * RET_QK_DIM
RET_V_W = RET_HEADS * RET_V_DIM
MLA_V_W = MLA_HEADS * MLA_V_DIM
MLA_PAD_DIM = 256
MLA_QK_W = MLA_HEADS * MLA_PAD_DIM

LANES = 128
VMEM_LIMIT = 56 * 1024 * 1024

F32 = jnp.float32
BF16 = jnp.bfloat16


def _params(*sem):
    return pltpu.CompilerParams(dimension_semantics=sem, vmem_limit_bytes=VMEM_LIMIT)


def _full(shape):
    nd = len(shape)
    return pl.BlockSpec(shape, lambda *_: (0,) * nd)


def _rms(x, gain):
    ms = jnp.mean(x * x, axis=-1, keepdims=True)
    return x * lax.rsqrt(ms + RMS_EPS) * gain


def _dot(a, b):
    return jnp.dot(a, b, preferred_element_type=F32)


def _dot_nt(a, b):
    return lax.dot_general(a, b, (((1,), (1,)), ((), ())), preferred_element_type=F32)


def _dot_tn(a, b):
    return lax.dot_general(a, b, (((0,), (0,)), ((), ())), preferred_element_type=F32)


def _rope_tables_kernel(pos_ref, inv_r_ref, inv_m_ref, mask_ref, cr_ref, sr_ref, cm_ref, sm_ref):
    pos = pos_ref[...]
    ang = pos * inv_r_ref[...]
    cr_ref[...] = jnp.cos(ang)
    sr_ref[...] = jnp.sin(ang)
    ang_m = pos * inv_m_ref[...]
    mask = mask_ref[...]
    cm_ref[...] = jnp.cos(ang_m) * mask
    sm_ref[...] = jnp.sin(ang_m) * mask


def _rope_tables(pos_f, bm=2048):
    T = pos_f.shape[0]
    half_r = RET_QK_DIM // 2
    half_m = MLA_ROPE_DIM // 2
    inv_r = ROPE_BASE ** (-jnp.arange(half_r, dtype=F32) / half_r)
    inv_m = ROPE_BASE ** (-jnp.arange(half_m, dtype=F32) / half_m)
    zeros = jnp.zeros((LANES - 2 * half_m,), F32)
    inv_m2 = jnp.concatenate([inv_m, inv_m, zeros])[None, :]
    mask = jnp.concatenate([jnp.ones((2 * half_m,), F32), zeros])[None, :]
    row = pl.BlockSpec((bm, LANES), lambda i: (i, 0))
    out = jax.ShapeDtypeStruct((T, LANES), F32)
    return pl.pallas_call(
        _rope_tables_kernel,
        grid=(T // bm,),
        in_specs=[pl.BlockSpec((bm, 1), lambda i: (i, 0)), _full((1, LANES)), _full((1, LANES)), _full((1, LANES))],
        out_specs=[row, row, row, row],
        out_shape=[out, out, out, out],
        compiler_params=_params("parallel"),
        name="rope_tables",
    )(pos_f, inv_r[None, :], inv_m2, mask)


def _proj_qk_kernel(x_ref, gain_ref, w_ref, cos_ref, sin_ref, o_ref, h_ref):
    j = pl.program_id(1)

    @pl.when(j == 0)
    def _():
        h_ref[...] = _rms(x_ref[...], gain_ref[...]).astype(BF16)

    acc = _dot(h_ref[...], w_ref[...])
    scale = jnp.where(j == 1, RET_QK_DIM ** -0.5, 1.0).astype(F32)
    cos = cos_ref[...]
    sin = sin_ref[...]
    half = RET_QK_DIM // 2
    for hd in range(RET_HEADS):
        lo = hd * RET_QK_DIM
        x1 = acc[:, lo:lo + half]
        x2 = acc[:, lo + half:lo + 2 * half]
        o_ref[:, lo:lo + half] = ((x1 * cos - x2 * sin) * scale).astype(BF16)
        o_ref[:, lo + half:lo + 2 * half] = ((x2 * cos + x1 * sin) * scale).astype(BF16)


def _proj_qk(x2d, gain, w, cos_r, sin_r, bm=1024):
    T = x2d.shape[0]
    bn = RET_QK_W
    return pl.pallas_call(
        _proj_qk_kernel,
        grid=(T // bm, 2),
        in_specs=[
            pl.BlockSpec((bm, D_MODEL), lambda i, j: (i, 0)),
            _full((1, D_MODEL)),
            pl.BlockSpec((D_MODEL, bn), lambda i, j: (0, j)),
            pl.BlockSpec((bm, LANES), lambda i, j: (i, 0)),
            pl.BlockSpec((bm, LANES), lambda i, j: (i, 0)),
        ],
        out_specs=pl.BlockSpec((bm, bn), lambda i, j: (i, j)),
        out_shape=jax.ShapeDtypeStruct((T, 2 * bn), BF16),
        scratch_shapes=[pltpu.VMEM((bm, D_MODEL), BF16)],
        compiler_params=_params("parallel", "arbitrary"),
        name="proj_ret_qk",
    )(x2d, gain, w, cos_r, sin_r)


def _proj_act_kernel(x_ref, gain_ref, w_ref, o_ref, h_ref):
    j = pl.program_id(1)

    @pl.when(j == 0)
    def _():
        h_ref[...] = _rms(x_ref[...], gain_ref[...]).astype(BF16)

    acc = _dot(h_ref[...], w_ref[...])

    @pl.when(j < 2)
    def _():
        o_ref[...] = acc.astype(BF16)

    @pl.when(jnp.logical_and(j >= 2, j < 4))
    def _():
        o_ref[...] = (acc * jax.nn.sigmoid(acc)).astype(BF16)

    @pl.when(j >= 4)
    def _():
        o_ref[...] = jax.nn.sigmoid(acc).astype(BF16)


def _proj_act(x2d, gain, w, bm=1024, bn=1024):
    T = x2d.shape[0]
    n = w.shape[1]
    return pl.pallas_call(
        _proj_act_kernel,
        grid=(T // bm, n // bn),
        in_specs=[
            pl.BlockSpec((bm, D_MODEL), lambda i, j: (i, 0)),
            _full((1, D_MODEL)),
            pl.BlockSpec((D_MODEL, bn), lambda i, j: (0, j)),
        ],
        out_specs=pl.BlockSpec((bm, bn), lambda i, j: (i, j)),
        out_shape=jax.ShapeDtypeStruct((T, n), BF16),
        scratch_shapes=[pltpu.VMEM((bm, D_MODEL), BF16)],
        compiler_params=_params("parallel", "arbitrary"),
        name="proj_v_gates",
    )(x2d, gain, w)


def _proj_lat_kernel(x_ref, gain_ref, w_ref, gq_ref, gkv_ref, cq_ref, ckv_ref, kr_ref):
    h = _rms(x_ref[...], gain_ref[...]).astype(BF16)
    acc = _dot(h, w_ref[...])
    cq_ref[...] = _rms(acc[:, :MLA_Q_LORA], gq_ref[...]).astype(BF16)
    ckv_ref[...] = _rms(acc[:, MLA_Q_LORA:MLA_Q_LORA + MLA_KV_LORA], gkv_ref[...]).astype(BF16)
    kr_ref[...] = acc[:, MLA_Q_LORA + MLA_KV_LORA:]


def _proj_lat(x2d, gain, w, gq, gkv, bm=1024):
    T = x2d.shape[0]
    n = w.shape[1]
    return pl.pallas_call(
        _proj_lat_kernel,
        grid=(T // bm,),
        in_specs=[
            pl.BlockSpec((bm, D_MODEL), lambda i: (i, 0)),
            _full((1, D_MODEL)),
            _full((D_MODEL, n)),
            _full((1, MLA_Q_LORA)),
            _full((1, MLA_KV_LORA)),
        ],
        out_specs=[
            pl.BlockSpec((bm, MLA_Q_LORA), lambda i: (i, 0)),
            pl.BlockSpec((bm, MLA_KV_LORA), lambda i: (i, 0)),
            pl.BlockSpec((bm, 2 * MLA_ROPE_DIM), lambda i: (i, 0)),
        ],
        out_shape=[
            jax.ShapeDtypeStruct((T, MLA_Q_LORA), BF16),
            jax.ShapeDtypeStruct((T, MLA_KV_LORA), BF16),
            jax.ShapeDtypeStruct((T, 2 * MLA_ROPE_DIM), F32),
        ],
        compiler_params=_params("parallel"),
        name="proj_latents",
    )(x2d, gain, w, gq, gkv)


RET_HEADS_PER_STEP = 2
RET_KERNEL_CHUNK = 256


def _retention_kernel(q_ref, k_ref, v_ref, g_ref, dec_ref, qd_ref, kd_ref, cd_ref, gain_ref, o_ref, state_ref, *, tb):
    @pl.when(pl.program_id(2) == 0)
    def _():
        state_ref[...] = jnp.zeros_like(state_ref)

    C, dk, dv, hp = RET_KERNEL_CHUNK, RET_QK_DIM, RET_V_DIM, RET_HEADS_PER_STEP
    n = tb // C

    def intra_scores(hd, c):
        rows = slice(c * C, (c + 1) * C)
        cols = slice(hd * dk, (hd + 1) * dk)
        return _dot_nt(q_ref[rows, cols], k_ref[rows, cols])

    scores = {(hd, 0): intra_scores(hd, 0) for hd in range(hp)}
    for c in range(n):
        rows = slice(c * C, (c + 1) * C)
        for hd in range(hp):
            qcols = slice(hd * dk, (hd + 1) * dk)
            vcols = slice(hd * dv, (hd + 1) * dv)
            q = q_ref[rows, qcols]
            v = v_ref[rows, vcols]
            st = state_ref[hd]
            p = (scores.pop((hd, c)) * dec_ref[hd]).astype(BF16)
            o = _dot(p, v) + _dot(q, st.astype(BF16)) * qd_ref[hd]
            kk = (k_ref[rows, qcols].astype(F32) * kd_ref[hd]).astype(BF16)
            kv = _dot_tn(kk, v)
            if hd == hp - 1 and c + 1 < n:
                for h2 in range(hp):
                    scores[(h2, c + 1)] = intra_scores(h2, c + 1)
            state_ref[hd] = st * cd_ref[hd, 0:1, 0:1] + kv
            on = _rms(o, gain_ref[:, vcols])
            o_ref[rows, vcols] = (on * g_ref[rows, vcols].astype(F32)).astype(BF16)


def _retention(qk, vg, ret_gain, B, S, tb=1024):
    T = qk.shape[0]
    H, C, dk, dv, hp = RET_HEADS, RET_KERNEL_CHUNK, RET_QK_DIM, RET_V_DIM, RET_HEADS_PER_STEP
    nt = S // tb
    log_g = jnp.log1p(-jnp.exp2(-5.0 - jnp.arange(H, dtype=F32)))
    idx = jnp.arange(C, dtype=F32)
    diff = idx[:, None] - idx[None, :]
    decay_in = jnp.where(diff >= 0, jnp.exp(jnp.maximum(diff, 0.0)[None] * log_g[:, None, None]), 0.0)
    q_dec = jnp.exp((idx + 1.0)[None, :] * log_g[:, None])
    k_dec = jnp.exp((C - 1.0 - idx)[None, :] * log_g[:, None])
    chunk_dec = jnp.exp(C * log_g)
    qd = jnp.broadcast_to(q_dec[:, :, None], (H, C, dv))
    kd = jnp.broadcast_to(k_dec[:, :, None], (H, C, dk))
    cd = jnp.broadcast_to(chunk_dec[:, None, None], (H, 8, LANES))

    def tab(d1, d2):
        return pl.BlockSpec((hp, d1, d2), lambda b, h, t: (h, 0, 0))

    ng = H // hp
    return pl.pallas_call(
        functools.partial(_retention_kernel, tb=tb),
        grid=(B, ng, nt),
        in_specs=[
            pl.BlockSpec((tb, hp * dk), lambda b, h, t: (b * nt + t, h)),
            pl.BlockSpec((tb, hp * dk), lambda b, h, t: (b * nt + t, ng + h)),
            pl.BlockSpec((tb, hp * dv), lambda b, h, t: (b * nt + t, h)),
            pl.BlockSpec((tb, hp * dv), lambda b, h, t: (b * nt + t, ng + h)),
            tab(C, C), tab(C, dv), tab(C, dk), tab(8, LANES),
            pl.BlockSpec((1, hp * dv), lambda b, h, t: (0, h)),
        ],
        out_specs=pl.BlockSpec((tb, hp * dv), lambda b, h, t: (b * nt + t, h)),
        out_shape=jax.ShapeDtypeStruct((T, RET_V_W), BF16),
        scratch_shapes=[pltpu.VMEM((hp, dk, dv), F32)],
        compiler_params=_params("parallel", "parallel", "arbitrary"),
        name="retention",
    )(qk, qk, vg, vg, decay_in, qd, kd, cd, ret_gain)


LOG2_E = 1.4426950408889634


def _mla_qkv_kernel(cq_ref, ckv_ref, kr_ref, cm_ref, sm_ref, wq_ref, wkn_ref, wvt_ref,
                    gq_ref, gkn_ref, gkr_ref, q_ref, k_ref, vt_ref):
    ckv = ckv_ref[...]
    yq = _dot(cq_ref[...], wq_ref[...])
    ykn = _dot(ckv, wkn_ref[...])
    vt_ref[...] = _dot_nt(wvt_ref[...], ckv).astype(BF16)
    cm = cm_ref[...]
    sm = sm_ref[...]
    first = (lax.broadcasted_iota(jnp.int32, (1, LANES), 1) < MLA_ROPE_DIM).astype(F32)
    inv_d = 1.0 / MLA_QK_DIM
    half = MLA_PAD_DIM // 2

    kr = kr_ref[...]
    ss_kr = jnp.sum(jnp.square(kr * first), axis=-1, keepdims=True)
    krg = kr * gkr_ref[...]
    k_roped = krg * cm + pltpu.roll(krg, MLA_ROPE_DIM, 1) * sm

    gq1 = gq_ref[:, :half]
    gq2 = gq_ref[:, half:]
    gkn = gkn_ref[...]
    for h in range(MLA_HEADS):
        lo = h * MLA_PAD_DIM
        y1 = yq[:, lo:lo + half]
        y2 = yq[:, lo + half:lo + 2 * half]
        ss = jnp.sum(y1 * y1, axis=-1, keepdims=True) + jnp.sum(jnp.square(y2 * first), axis=-1, keepdims=True)
        sc = lax.rsqrt(ss * inv_d + RMS_EPS) * (MLA_QK_DIM ** -0.5 * LOG2_E)
        y2g = y2 * gq2 * sc
        q_ref[:, lo:lo + half] = (y1 * gq1 * sc).astype(BF16)
        q_ref[:, lo + half:lo + 2 * half] = (y2g * cm + pltpu.roll(y2g, MLA_ROPE_DIM, 1) * sm).astype(BF16)
        kn = ykn[:, h * half:(h + 1) * half]
        ssk = jnp.sum(kn * kn, axis=-1, keepdims=True) + ss_kr
        sck = lax.rsqrt(ssk * inv_d + RMS_EPS)
        k_ref[:, lo:lo + half] = (kn * gkn * sck).astype(BF16)
        k_ref[:, lo + half:lo + 2 * half] = (k_roped * sck).astype(BF16)


def _mla_qkv(cq, ckv, kr, cm, sm, wq, wkn, wvt, gq, gkn, gkr, bm=512):
    T = cq.shape[0]

    def row(w):
        return pl.BlockSpec((bm, w), lambda i: (i, 0))

    return pl.pallas_call(
        _mla_qkv_kernel,
        grid=(T // bm,),
        in_specs=[row(MLA_Q_LORA), row(MLA_KV_LORA), row(LANES), row(LANES), row(LANES),
                  _full(wq.shape), _full(wkn.shape), _full(wvt.shape),
                  _full(gq.shape), _full(gkn.shape), _full(gkr.shape)],
        out_specs=[row(MLA_QK_W), row(MLA_QK_W), pl.BlockSpec((MLA_V_W, bm), lambda i: (0, i))],
        out_shape=[jax.ShapeDtypeStruct((T, MLA_QK_W), BF16),
                   jax.ShapeDtypeStruct((T, MLA_QK_W), BF16),
                   jax.ShapeDtypeStruct((MLA_V_W, T), BF16)],
        compiler_params=_params("parallel"),
        name="mla_qkv",
    )(cq, ckv, kr, cm, sm, wq, wkn, wvt, gq, gkn, gkr)


ATTN_HEADS_PER_STEP = 2


def _attn_kernel(q_ref, k_ref, vt_ref, o_ref, m_ref, l_ref, acc_ref, s_ref, bm_ref, *, tq):
    qi = pl.program_id(2)
    m_ref[...] = jnp.full_like(m_ref, -1e30)
    l_ref[...] = jnp.zeros_like(l_ref)
    acc_ref[...] = jnp.zeros_like(acc_ref)

    def scores(c, kj):
        r0 = pl.multiple_of(kj * tq, tq)
        cols = slice(c * MLA_PAD_DIM, (c + 1) * MLA_PAD_DIM)
        return _dot_nt(k_ref[pl.ds(r0, tq), cols], q_ref[:, cols])

    def park(kj):
        st = scores(0, kj)
        s_ref[...] = st
        bm_ref[...] = jnp.max(st, axis=0, keepdims=True)

    def update(c, kj, st, masked, block_max=None):
        r0 = pl.multiple_of(kj * tq, tq)
        if masked:
            kpos = lax.broadcasted_iota(jnp.int32, (tq, tq), 0)
            qpos = lax.broadcasted_iota(jnp.int32, (tq, tq), 1)
            st = jnp.where(kpos <= qpos, st, -1e30)
        if block_max is None:
            block_max = jnp.max(st, axis=0, keepdims=True)
        rows = slice(c * MLA_V_DIM, (c + 1) * MLA_V_DIM)
        m_prev = m_ref[c:c + 1, :]
        m_new = jnp.maximum(m_prev, block_max)
        alpha = jnp.exp2(m_prev - m_new)
        pt = jnp.exp2(st - m_new)
        l_ref[c:c + 1, :] = alpha * l_ref[c:c + 1, :] + jnp.sum(pt, axis=0, keepdims=True)
        acc_ref[rows, :] = alpha * acc_ref[rows, :] + _dot(vt_ref[rows, pl.ds(r0, tq)], pt.astype(BF16))
        m_ref[c:c + 1, :] = m_new

    def step(kj, masked):
        s1 = scores(1, kj)
        update(0, kj, s_ref[...], masked, None if masked else bm_ref[...])
        if not masked:
            park(kj + 1)
        update(1, kj, s1, masked)

    def body(kj, carry):
        step(kj, False)
        return carry

    park(0)
    lax.fori_loop(0, qi, body, 0)
    step(qi, True)
    for c in range(ATTN_HEADS_PER_STEP):
        rows = slice(c * MLA_V_DIM, (c + 1) * MLA_V_DIM)
        o_ref[rows, :] = (acc_ref[rows, :] / l_ref[c:c + 1, :]).astype(BF16)


def _attention(q, k, vt, B, S, tq=512):
    T = q.shape[0]
    nq = S // tq
    hp = ATTN_HEADS_PER_STEP
    return pl.pallas_call(
        functools.partial(_attn_kernel, tq=tq),
        grid=(B, MLA_HEADS // hp, nq),
        in_specs=[
            pl.BlockSpec((tq, hp * MLA_PAD_DIM), lambda b, h, i: (b * nq + i, h)),
            pl.BlockSpec((S, hp * MLA_PAD_DIM), lambda b, h, i: (b, h)),
            pl.BlockSpec((hp * MLA_V_DIM, S), lambda b, h, i: (h, b)),
        ],
        out_specs=pl.BlockSpec((hp * MLA_V_DIM, tq), lambda b, h, i: (h, b * nq + i)),
        out_shape=jax.ShapeDtypeStruct((MLA_V_W, T), BF16),
        scratch_shapes=[pltpu.VMEM((hp, tq), F32), pltpu.VMEM((hp, tq), F32),
                        pltpu.VMEM((hp * MLA_V_DIM, tq), F32), pltpu.VMEM((tq, tq), F32),
                        pltpu.VMEM((1, tq), F32)],
        compiler_params=_params("parallel", "parallel", "arbitrary"),
        name="mla_attention",
    )(q, k, vt)


def _merge_kernel(x_ref, ro_ref, mot_ref, gr_ref, gm_ref, wbr_ref, wbm_ref, wo_ref, o_ref):
    a_ret = _dot(ro_ref[...], wbr_ref[...])
    a_mla = _dot_tn(mot_ref[...], wbm_ref[...])
    mixed = gr_ref[...].astype(F32) * a_ret + gm_ref[...].astype(F32) * a_mla
    o_ref[...] = x_ref[...] + _dot(mixed.astype(BF16), wo_ref[...])


def _merge(x2d, ro, mo, vg, wbr, wbm, wo, bm=512):
    T = x2d.shape[0]
    gate0 = (2 * RET_V_W) // D_MODEL
    return pl.pallas_call(
        _merge_kernel,
        grid=(T // bm,),
        in_specs=[
            pl.BlockSpec((bm, D_MODEL), lambda i: (i, 0)),
            pl.BlockSpec((bm, RET_V_W), lambda i: (i, 0)),
            pl.BlockSpec((MLA_V_W, bm), lambda i: (0, i)),
            pl.BlockSpec((bm, D_MODEL), lambda i: (i, gate0)),
            pl.BlockSpec((bm, D_MODEL), lambda i: (i, gate0 + 1)),
            _full(wbr.shape), _full(wbm.shape), _full(wo.shape),
        ],
        out_specs=pl.BlockSpec((bm, D_MODEL), lambda i: (i, 0)),
        out_shape=jax.ShapeDtypeStruct((T, D_MODEL), F32),
        compiler_params=_params("parallel"),
        name="merge_out_proj",
    )(x2d, ro, mo, vg, vg, wbr, wbm, wo)


def _mlp_kernel(x_ref, gain_ref, wup_ref, wdown_ref, o_ref, *, bf):
    x = x_ref[...]
    h = _rms(x, gain_ref[...]).astype(BF16)
    acc = x
    for c in range(D_FF // bf):
        up = _dot(h, wup_ref[:, c * bf:(c + 1) * bf])
        a = jnp.square(jnp.maximum(up, 0.0)).astype(BF16)
        acc = acc + _dot(a, wdown_ref[c * bf:(c + 1) * bf, :])
    o_ref[...] = acc


def _mlp(x2d, gain, wup, wdown, bm=512, bf=1024):
    T = x2d.shape[0]
    return pl.pallas_call(
        functools.partial(_mlp_kernel, bf=bf),
        grid=(T // bm,),
        in_specs=[pl.BlockSpec((bm, D_MODEL), lambda i: (i, 0)), _full((1, D_MODEL)),
                  _full(wup.shape), _full(wdown.shape)],
        out_specs=pl.BlockSpec((bm, D_MODEL), lambda i: (i, 0)),
        out_shape=jax.ShapeDtypeStruct((T, D_MODEL), F32),
        compiler_params=_params("parallel"),
        name="mlp",
    )(x2d, gain, wup, wdown)


def _ple_kernel(x_ref, p_ref, gain_ref, wg_ref, wp_ref, o_ref):
    x = x_ref[...]
    h = _rms(x, gain_ref[...]).astype(BF16)
    gate = jax.nn.sigmoid(_dot(h, wg_ref[...]))
    o_ref[...] = x + gate * _dot(p_ref[...].astype(BF16), wp_ref[...])


def _ple(x2d, p2d, gain, wg, wp, bm=1024):
    T = x2d.shape[0]
    return pl.pallas_call(
        _ple_kernel,
        grid=(T // bm,),
        in_specs=[pl.BlockSpec((bm, D_MODEL), lambda i: (i, 0)), pl.BlockSpec((bm, PLE_DIM), lambda i: (i, 0)),
                  _full((1, D_MODEL)), _full(wg.shape), _full(wp.shape)],
        out_specs=pl.BlockSpec((bm, D_MODEL), lambda i: (i, 0)),
        out_shape=jax.ShapeDtypeStruct((T, D_MODEL), F32),
        compiler_params=_params("parallel"),
        name="ple",
    )(x2d, p2d, gain, wg, wp)


def _rotate_half_cols(w):
    half = w.shape[-1] // 2
    return jnp.concatenate([-w[..., half:], w[..., :half]], axis=-1)


def _swap_halves(g):
    half = g.shape[-1] // 2
    return jnp.concatenate([g[..., half:], g[..., :half]], axis=-1)


def _layer(x2d, p2d, tables, B, S, norm_mix, w_in, ret_norm, q_lat_norm, kv_lat_norm, w_uq, w_ukv,
           q_norm, k_norm, w_br, w_bm, w_o, norm_mlp, w_up, w_down, norm_ple, w_ple_gate, w_ple):
    cos_r, sin_r, cos_m, sin_m = tables
    o_rv = 2 * RET_QK_W
    o_cq = o_rv + 2 * RET_V_W
    o_ckv = o_cq + MLA_Q_LORA
    o_kr = o_ckv + MLA_KV_LORA
    o_gr = o_kr + MLA_ROPE_DIM

    gain_mix = norm_mix[None, :]
    w_qk = w_in[:, :o_rv].astype(BF16)
    w_vg = jnp.concatenate([w_in[:, o_rv:o_cq], w_in[:, o_gr:]], axis=1).astype(BF16)
    w_kr = w_in[:, o_kr:o_gr]
    w_lat = jnp.concatenate([w_in[:, o_cq:o_gr], _rotate_half_cols(w_kr)], axis=1).astype(BF16)

    qk = _proj_qk(x2d, gain_mix, w_qk, cos_r, sin_r)
    vg = _proj_act(x2d, gain_mix, w_vg)
    cq, ckv, kr = _proj_lat(x2d, gain_mix, w_lat, q_lat_norm[None, :], kv_lat_norm[None, :])

    ro = _retention(qk, vg, ret_norm[None, :], B, S)

    wq = w_uq.reshape(MLA_Q_LORA, MLA_HEADS, MLA_QK_DIM)
    wq_rope = wq[:, :, MLA_NOPE_DIM:]
    wq = jnp.concatenate([wq, _rotate_half_cols(wq_rope)], axis=-1).reshape(MLA_Q_LORA, MLA_QK_W).astype(BF16)
    wkv = w_ukv.reshape(MLA_KV_LORA, MLA_HEADS, MLA_NOPE_DIM + MLA_V_DIM)
    wkn = wkv[:, :, :MLA_NOPE_DIM].reshape(MLA_KV_LORA, MLA_HEADS * MLA_NOPE_DIM).astype(BF16)
    wvt = wkv[:, :, MLA_NOPE_DIM:].reshape(MLA_KV_LORA, MLA_V_W).T.astype(BF16)
    gq = jnp.concatenate([q_norm, _swap_halves(q_norm[MLA_NOPE_DIM:])])[None, :]
    gkn = k_norm[None, :MLA_NOPE_DIM]
    gkr = jnp.concatenate([k_norm[MLA_NOPE_DIM:], _swap_halves(k_norm[MLA_NOPE_DIM:])])[None, :]
    q, k, vt = _mla_qkv(cq, ckv, kr, cos_m, sin_m, wq, wkn, wvt, gq, gkn, gkr)
    mo = _attention(q, k, vt, B, S)

    x2d = _merge(x2d, ro, mo, vg, w_br.astype(BF16), w_bm.astype(BF16), w_o.astype(BF16))
    x2d = _mlp(x2d, norm_mlp[None, :], w_up.astype(BF16), w_down.astype(BF16))
    x2d = _ple(x2d, p2d, norm_ple[None, :], w_ple_gate.astype(BF16), w_ple.astype(BF16))
    return x2d


def kernel(x, p, positions, norm_mix, w_in, ret_norm, q_lat_norm, kv_lat_norm, w_uq, w_ukv, q_norm, k_norm,
           w_br, w_bm, w_o, norm_mlp, w_up, w_down, norm_ple, w_ple_gate, w_ple):
    B, S, D = x.shape
    T = B * S
    x2d = x.reshape(T, D)
    pos_f = positions.astype(F32).reshape(T, 1)
    tables = _rope_tables(pos_f)
    for i in range(p.shape[0]):
        x2d = _layer(x2d, p[i].reshape(T, PLE_DIM), tables, B, S, norm_mix[i], w_in[i], ret_norm[i],
                     q_lat_norm[i], kv_lat_norm[i], w_uq[i], w_ukv[i], q_norm[i], k_norm[i], w_br[i], w_bm[i],
                     w_o[i], norm_mlp[i], w_up[i], w_down[i], norm_ple[i], w_ple_gate[i], w_ple[i])
    return x2d.reshape(B, S, D)
```

```python
import functools

import jax
import jax.numpy as jnp
from jax import lax
from jax.experimental import pallas as pl
from jax.experimental.pallas import tpu as pltpu

D_MODEL = 1024
PLE_DIM = 256
RET_HEADS = 4
RET_QK_DIM = 256
RET_V_DIM = 512
MLA_HEADS = 8
MLA_NOPE_DIM = 128
MLA_ROPE_DIM = 64
MLA_QK_DIM = MLA_NOPE_DIM + MLA_ROPE_DIM
MLA_V_DIM = 128
MLA_Q_LORA = 384
MLA_KV_LORA = 256
D_FF = 4 * D_MODEL
ROPE_BASE = 10000.0
RMS_EPS = 1e-6

RET_QK_W = RET_HEADS * RET_QK_DIM
RET_V_W = RET_HEADS * RET_V_DIM
MLA_V_W = MLA_HEADS * MLA_V_DIM
MLA_PAD_DIM = 256
MLA_QK_W = MLA_HEADS * MLA_PAD_DIM

LANES = 128
VMEM_LIMIT = 56 * 1024 * 1024
LOG2_E = 1.4426950408889634

F32 = jnp.float32
BF16 = jnp.bfloat16


def _params(*sem):
    return pltpu.CompilerParams(dimension_semantics=sem, vmem_limit_bytes=VMEM_LIMIT)


def _full(shape):
    nd = len(shape)
    return pl.BlockSpec(shape, lambda *_: (0,) * nd)


def _resident(shape):
    nd = len(shape)
    return pl.BlockSpec(shape, lambda *_: (0,) * nd, pipeline_mode=pl.Buffered(1))


def _rms(x, gain):
    ms = jnp.mean(x * x, axis=-1, keepdims=True)
    return x * lax.rsqrt(ms + RMS_EPS) * gain


def _dot(a, b):
    return jnp.dot(a, b, preferred_element_type=F32)


def _dot_nt(a, b):
    return lax.dot_general(a, b, (((1,), (1,)), ((), ())), preferred_element_type=F32)


def _dot_tn(a, b):
    return lax.dot_general(a, b, (((0,), (0,)), ((), ())), preferred_element_type=F32)


def _rope_tables_kernel(pos_ref, inv_r_ref, inv_m_ref, mask_ref, cr_ref, sr_ref, cm_ref, sm_ref):
    pos = pos_ref[...]
    ang = pos * inv_r_ref[...]
    cr_ref[...] = jnp.cos(ang)
    sr_ref[...] = jnp.sin(ang)
    ang_m = pos * inv_m_ref[...]
    mask = mask_ref[...]
    cm_ref[...] = jnp.cos(ang_m) * mask
    sm_ref[...] = jnp.sin(ang_m) * mask


def _rope_tables(pos_f, bm=2048):
    T = pos_f.shape[0]
    half_r = RET_QK_DIM // 2
    half_m = MLA_ROPE_DIM // 2
    inv_r = ROPE_BASE ** (-jnp.arange(half_r, dtype=F32) / half_r)
    inv_m = ROPE_BASE ** (-jnp.arange(half_m, dtype=F32) / half_m)
    zeros = jnp.zeros((LANES - 2 * half_m,), F32)
    inv_m2 = jnp.concatenate([inv_m, inv_m, zeros])[None, :]
    mask = jnp.concatenate([jnp.ones((2 * half_m,), F32), zeros])[None, :]
    row = pl.BlockSpec((bm, LANES), lambda i: (i, 0))
    out = jax.ShapeDtypeStruct((T, LANES), F32)
    return pl.pallas_call(
        _rope_tables_kernel,
        grid=(T // bm,),
        in_specs=[pl.BlockSpec((bm, 1), lambda i: (i, 0)), _full((1, LANES)), _full((1, LANES)), _full((1, LANES))],
        out_specs=[row, row, row, row],
        out_shape=[out, out, out, out],
        compiler_params=_params("parallel"),
        name="rope_tables",
    )(pos_f, inv_r[None, :], inv_m2, mask)


PROJ_VG_BLOCK = 1024
PROJ_VG_W = 2 * RET_V_W + 2 * D_MODEL
PROJ_LAT_W = MLA_Q_LORA + MLA_KV_LORA + 2 * MLA_ROPE_DIM


def _proj_kernel(x_ref, gain_ref, w_ref, cos_ref, sin_ref, gq_ref, gkv_ref, qk_ref, vg_ref, cq_ref, ckv_ref, kr_ref):
    h = _rms(x_ref[...], gain_ref[...]).astype(BF16)
    cos = cos_ref[...]
    sin = sin_ref[...]
    half = RET_QK_DIM // 2
    for j in range(2):
        acc = _dot(h, w_ref[:, j * RET_QK_W:(j + 1) * RET_QK_W])
        for hd in range(RET_HEADS):
            lo = hd * RET_QK_DIM
            x1 = acc[:, lo:lo + half]
            x2 = acc[:, lo + half:lo + 2 * half]
            r1 = x1 * cos - x2 * sin
            r2 = x2 * cos + x1 * sin
            if j == 1:
                r1 = r1 * (RET_QK_DIM ** -0.5)
                r2 = r2 * (RET_QK_DIM ** -0.5)
            qk_ref[:, j * RET_QK_W + lo:j * RET_QK_W + lo + half] = r1.astype(BF16)
            qk_ref[:, j * RET_QK_W + lo + half:j * RET_QK_W + lo + 2 * half] = r2.astype(BF16)
    base = 2 * RET_QK_W
    bn = PROJ_VG_BLOCK
    for j in range(PROJ_VG_W // bn):
        acc = _dot(h, w_ref[:, base + j * bn:base + (j + 1) * bn])
        if j * bn >= 2 * RET_V_W:
            acc = jax.nn.sigmoid(acc)
        elif j * bn >= RET_V_W:
            acc = acc * jax.nn.sigmoid(acc)
        vg_ref[:, j * bn:(j + 1) * bn] = acc.astype(BF16)
    base += PROJ_VG_W
    acc = _dot(h, w_ref[:, base:base + PROJ_LAT_W])
    cq_ref[...] = _rms(acc[:, :MLA_Q_LORA], gq_ref[...]).astype(BF16)
    ckv_ref[...] = _rms(acc[:, MLA_Q_LORA:MLA_Q_LORA + MLA_KV_LORA], gkv_ref[...]).astype(BF16)
    kr_ref[...] = acc[:, MLA_Q_LORA + MLA_KV_LORA:]


def _proj(x2d, gain, w, cos_r, sin_r, gq, gkv, bm=512):
    T = x2d.shape[0]

    def row(n):
        return pl.BlockSpec((bm, n), lambda i: (i, 0))

    return pl.pallas_call(
        _proj_kernel,
        grid=(T // bm,),
        in_specs=[row(D_MODEL), _full((1, D_MODEL)), _resident(w.shape), row(LANES), row(LANES),
                  _full((1, MLA_Q_LORA)), _full((1, MLA_KV_LORA))],
        out_specs=[row(2 * RET_QK_W), row(PROJ_VG_W), row(MLA_Q_LORA), row(MLA_KV_LORA), row(2 * MLA_ROPE_DIM)],
        out_shape=[
            jax.ShapeDtypeStruct((T, 2 * RET_QK_W), BF16),
            jax.ShapeDtypeStruct((T, PROJ_VG_W), BF16),
            jax.ShapeDtypeStruct((T, MLA_Q_LORA), BF16),
            jax.ShapeDtypeStruct((T, MLA_KV_LORA), BF16),
            jax.ShapeDtypeStruct((T, 2 * MLA_ROPE_DIM), F32),
        ],
        compiler_params=_params("parallel"),
        name="proj",
    )(x2d, gain, w, cos_r, sin_r, gq, gkv)


RET_HEADS_PER_STEP = 2
RET_KERNEL_CHUNK = 256


def _retention_kernel(q_ref, k_ref, v_ref, g_ref, dec_ref, qd_ref, kd_ref, cd_ref, gain_ref, o_ref, state_ref, *, tb):
    @pl.when(pl.program_id(2) == 0)
    def _():
        state_ref[...] = jnp.zeros_like(state_ref)

    C, dk, dv, hp = RET_KERNEL_CHUNK, RET_QK_DIM, RET_V_DIM, RET_HEADS_PER_STEP
    n = tb // C

    def intra_scores(hd, c):
        rows = slice(c * C, (c + 1) * C)
        cols = slice(hd * dk, (hd + 1) * dk)
        return _dot_nt(q_ref[rows, cols], k_ref[rows, cols])

    scores = {(hd, 0): intra_scores(hd, 0) for hd in range(hp)}
    for c in range(n):
        rows = slice(c * C, (c + 1) * C)
        for hd in range(hp):
            qcols = slice(hd * dk, (hd + 1) * dk)
            vcols = slice(hd * dv, (hd + 1) * dv)
            q = q_ref[rows, qcols]
            v = v_ref[rows, vcols]
            st = state_ref[hd]
            p = (scores.pop((hd, c)) * dec_ref[hd]).astype(BF16)
            o = _dot(p, v) + _dot(q, st.astype(BF16)) * qd_ref[hd]
            kk = (k_ref[rows, qcols].astype(F32) * kd_ref[hd]).astype(BF16)
            kv = _dot_tn(kk, v)
            if hd == hp - 1 and c + 1 < n:
                for h2 in range(hp):
                    scores[(h2, c + 1)] = intra_scores(h2, c + 1)
            state_ref[hd] = st * cd_ref[hd, 0:1, 0:1] + kv
            on = _rms(o, gain_ref[:, vcols])
            o_ref[rows, vcols] = (on * g_ref[rows, vcols].astype(F32)).astype(BF16)


def _retention(qk, vg, ret_gain, B, S, tb=1024):
    T = qk.shape[0]
    H, C, dk, dv, hp = RET_HEADS, RET_KERNEL_CHUNK, RET_QK_DIM, RET_V_DIM, RET_HEADS_PER_STEP
    nt = S // tb
    log_g = jnp.log1p(-jnp.exp2(-5.0 - jnp.arange(H, dtype=F32)))
    idx = jnp.arange(C, dtype=F32)
    diff = idx[:, None] - idx[None, :]
    decay_in = jnp.where(diff >= 0, jnp.exp(jnp.maximum(diff, 0.0)[None] * log_g[:, None, None]), 0.0)
    q_dec = jnp.exp((idx + 1.0)[None, :] * log_g[:, None])
    k_dec = jnp.exp((C - 1.0 - idx)[None, :] * log_g[:, None])
    chunk_dec = jnp.exp(C * log_g)
    qd = jnp.broadcast_to(q_dec[:, :, None], (H, C, dv))
    kd = jnp.broadcast_to(k_dec[:, :, None], (H, C, dk))
    cd = jnp.broadcast_to(chunk_dec[:, None, None], (H, 8, LANES))

    def tab(d1, d2):
        return pl.BlockSpec((hp, d1, d2), lambda b, h, t: (h, 0, 0))

    ng = H // hp
    return pl.pallas_call(
        functools.partial(_retention_kernel, tb=tb),
        grid=(B, ng, nt),
        in_specs=[
            pl.BlockSpec((tb, hp * dk), lambda b, h, t: (b * nt + t, h)),
            pl.BlockSpec((tb, hp * dk), lambda b, h, t: (b * nt + t, ng + h)),
            pl.BlockSpec((tb, hp * dv), lambda b, h, t: (b * nt + t, h)),
            pl.BlockSpec((tb, hp * dv), lambda b, h, t: (b * nt + t, ng + h)),
            tab(C, C), tab(C, dv), tab(C, dk), tab(8, LANES),
            pl.BlockSpec((1, hp * dv), lambda b, h, t: (0, h)),
        ],
        out_specs=pl.BlockSpec((tb, hp * dv), lambda b, h, t: (b * nt + t, h)),
        out_shape=jax.ShapeDtypeStruct((T, RET_V_W), BF16),
        scratch_shapes=[pltpu.VMEM((hp, dk, dv), F32)],
        compiler_params=_params("parallel", "parallel", "arbitrary"),
        name="retention",
    )(qk, qk, vg, vg, decay_in, qd, kd, cd, ret_gain)


def _mla_qkv_kernel(cq_ref, ckv_ref, kr_ref, cm_ref, sm_ref, wq_ref, wkn_ref, wvt_ref,
                    gq_ref, gkn_ref, gkr_ref, q_ref, k_ref, vt_ref):
    ckv = ckv_ref[...]
    yq = _dot(cq_ref[...], wq_ref[...])
    ykn = _dot(ckv, wkn_ref[...])
    vt_ref[...] = _dot_nt(wvt_ref[...], ckv).astype(BF16)
    cm = cm_ref[...]
    sm = sm_ref[...]
    first = (lax.broadcasted_iota(jnp.int32, (1, LANES), 1) < MLA_ROPE_DIM).astype(F32)
    inv_d = 1.0 / MLA_QK_DIM
    half = MLA_PAD_DIM // 2

    kr = kr_ref[...]
    ss_kr = jnp.sum(jnp.square(kr * first), axis=-1, keepdims=True)
    krg = kr * gkr_ref[...]
    k_roped = krg * cm + pltpu.roll(krg, MLA_ROPE_DIM, 1) * sm

    gq1 = gq_ref[:, :half]
    gq2 = gq_ref[:, half:]
    gkn = gkn_ref[...]
    for h in range(MLA_HEADS):
        lo = h * MLA_PAD_DIM
        y1 = yq[:, lo:lo + half]
        y2 = yq[:, lo + half:lo + 2 * half]
        ss = jnp.sum(y1 * y1, axis=-1, keepdims=True) + jnp.sum(jnp.square(y2 * first), axis=-1, keepdims=True)
        sc = lax.rsqrt(ss * inv_d + RMS_EPS) * (MLA_QK_DIM ** -0.5 * LOG2_E)
        y2g = y2 * gq2 * sc
        q_ref[:, lo:lo + half] = (y1 * gq1 * sc).astype(BF16)
        q_ref[:, lo + half:lo + 2 * half] = (y2g * cm + pltpu.roll(y2g, MLA_ROPE_DIM, 1) * sm).astype(BF16)
        kn = ykn[:, h * half:(h + 1) * half]
        ssk = jnp.sum(kn * kn, axis=-1, keepdims=True) + ss_kr
        sck = lax.rsqrt(ssk * inv_d + RMS_EPS)
        k_ref[:, lo:lo + half] = (kn * gkn * sck).astype(BF16)
        k_ref[:, lo + half:lo + 2 * half] = (k_roped * sck).astype(BF16)


def _mla_qkv(cq, ckv, kr, cm, sm, wq, wkn, wvt, gq, gkn, gkr, bm=512):
    T = cq.shape[0]

    def row(w):
        return pl.BlockSpec((bm, w), lambda i: (i, 0))

    return pl.pallas_call(
        _mla_qkv_kernel,
        grid=(T // bm,),
        in_specs=[row(MLA_Q_LORA), row(MLA_KV_LORA), row(LANES), row(LANES), row(LANES),
                  _full(wq.shape), _full(wkn.shape), _full(wvt.shape),
                  _full(gq.shape), _full(gkn.shape), _full(gkr.shape)],
        out_specs=[row(MLA_QK_W), row(MLA_QK_W), pl.BlockSpec((MLA_V_W, bm), lambda i: (0, i))],
        out_shape=[jax.ShapeDtypeStruct((T, MLA_QK_W), BF16),
                   jax.ShapeDtypeStruct((T, MLA_QK_W), BF16),
                   jax.ShapeDtypeStruct((MLA_V_W, T), BF16)],
        compiler_params=_params("parallel"),
        name="mla_qkv",
    )(cq, ckv, kr, cm, sm, wq, wkn, wvt, gq, gkn, gkr)


ATTN_HEADS_PER_STEP = 2


def _attn_kernel(q_ref, k_ref, vt_ref, o_ref, m_ref, l_ref, acc_ref, s_ref, bm_ref, *, tq):
    qi = pl.program_id(2)
    m_ref[...] = jnp.full_like(m_ref, -1e30)
    l_ref[...] = jnp.zeros_like(l_ref)
    acc_ref[...] = jnp.zeros_like(acc_ref)

    def scores(c, kj):
        r0 = pl.multiple_of(kj * tq, tq)
        cols = slice(c * MLA_PAD_DIM, (c + 1) * MLA_PAD_DIM)
        return _dot_nt(k_ref[pl.ds(r0, tq), cols], q_ref[:, cols])

    def park(kj):
        st = scores(0, kj)
        s_ref[...] = st
        bm_ref[...] = jnp.max(st, axis=0, keepdims=True)

    def update(c, kj, st, masked, block_max=None):
        r0 = pl.multiple_of(kj * tq, tq)
        if masked:
            kpos = lax.broadcasted_iota(jnp.int32, (tq, tq), 0)
            qpos = lax.broadcasted_iota(jnp.int32, (tq, tq), 1)
            st = jnp.where(kpos <= qpos, st, -1e30)
        if block_max is None:
            block_max = jnp.max(st, axis=0, keepdims=True)
        rows = slice(c * MLA_V_DIM, (c + 1) * MLA_V_DIM)
        m_prev = m_ref[c:c + 1, :]
        m_new = jnp.maximum(m_prev, block_max)
        alpha = jnp.exp2(m_prev - m_new)
        pt = jnp.exp2(st - m_new)
        l_ref[c:c + 1, :] = alpha * l_ref[c:c + 1, :] + jnp.sum(pt, axis=0, keepdims=True)
        acc_ref[rows, :] = alpha * acc_ref[rows, :] + _dot(vt_ref[rows, pl.ds(r0, tq)], pt.astype(BF16))
        m_ref[c:c + 1, :] = m_new

    def step(kj, masked):
        s1 = scores(1, kj)
        update(0, kj, s_ref[...], masked, None if masked else bm_ref[...])
        if not masked:
            park(kj + 1)
        update(1, kj, s1, masked)

    def body(kj, carry):
        step(kj, False)
        return carry

    park(0)
    lax.fori_loop(0, qi, body, 0)
    step(qi, True)
    for c in range(ATTN_HEADS_PER_STEP):
        rows = slice(c * MLA_V_DIM, (c + 1) * MLA_V_DIM)
        o_ref[rows, :] = (acc_ref[rows, :] / l_ref[c:c + 1, :]).astype(BF16)


def _attention(q, k, vt, B, S, tq=512):
    T = q.shape[0]
    nq = S // tq
    hp = ATTN_HEADS_PER_STEP
    return pl.pallas_call(
        functools.partial(_attn_kernel, tq=tq),
        grid=(B, MLA_HEADS // hp, nq),
        in_specs=[
            pl.BlockSpec((tq, hp * MLA_PAD_DIM), lambda b, h, i: (b * nq + i, h)),
            pl.BlockSpec((S, hp * MLA_PAD_DIM), lambda b, h, i: (b, h)),
            pl.BlockSpec((hp * MLA_V_DIM, S), lambda b, h, i: (h, b)),
        ],
        out_specs=pl.BlockSpec((hp * MLA_V_DIM, tq), lambda b, h, i: (h, b * nq + i)),
        out_shape=jax.ShapeDtypeStruct((MLA_V_W, T), BF16),
        scratch_shapes=[pltpu.VMEM((hp, tq), F32), pltpu.VMEM((hp, tq), F32),
                        pltpu.VMEM((hp * MLA_V_DIM, tq), F32), pltpu.VMEM((tq, tq), F32),
                        pltpu.VMEM((1, tq), F32)],
        compiler_params=_params("parallel", "parallel", "arbitrary"),
        name="mla_attention",
    )(q, k, vt)


def _merge_kernel(x_ref, ro_ref, mot_ref, gr_ref, gm_ref, wbr_ref, wbm_ref, wo_ref, o_ref):
    a_ret = _dot(ro_ref[...], wbr_ref[...])
    a_mla = _dot_tn(mot_ref[...], wbm_ref[...])
    mixed = gr_ref[...].astype(F32) * a_ret + gm_ref[...].astype(F32) * a_mla
    o_ref[...] = x_ref[...] + _dot(mixed.astype(BF16), wo_ref[...])


def _merge(x2d, ro, mo, vg, wbr, wbm, wo, bm=512):
    T = x2d.shape[0]
    gate0 = (2 * RET_V_W) // D_MODEL
    return pl.pallas_call(
        _merge_kernel,
        grid=(T // bm,),
        in_specs=[
            pl.BlockSpec((bm, D_MODEL), lambda i: (i, 0)),
            pl.BlockSpec((bm, RET_V_W), lambda i: (i, 0)),
            pl.BlockSpec((MLA_V_W, bm), lambda i: (0, i)),
            pl.BlockSpec((bm, D_MODEL), lambda i: (i, gate0)),
            pl.BlockSpec((bm, D_MODEL), lambda i: (i, gate0 + 1)),
            _resident(wbr.shape), _resident(wbm.shape), _resident(wo.shape),
        ],
        out_specs=pl.BlockSpec((bm, D_MODEL), lambda i: (i, 0)),
        out_shape=jax.ShapeDtypeStruct((T, D_MODEL), F32),
        compiler_params=_params("parallel"),
        name="merge_out_proj",
    )(x2d, ro, mo, vg, vg, wbr, wbm, wo)


def _mlp_kernel(x_ref, gain_ref, wup_ref, wdown_ref, o_ref, *, bf):
    x = x_ref[...]
    h = _rms(x, gain_ref[...]).astype(BF16)
    acc = x
    for c in range(D_FF // bf):
        up = _dot(h, wup_ref[:, c * bf:(c + 1) * bf])
        a = jnp.square(jnp.maximum(up, 0.0)).astype(BF16)
        acc = acc + _dot(a, wdown_ref[c * bf:(c + 1) * bf, :])
    o_ref[...] = acc


def _mlp(x2d, gain, wup, wdown, bm=512, bf=1024):
    T = x2d.shape[0]
    return pl.pallas_call(
        functools.partial(_mlp_kernel, bf=bf),
        grid=(T // bm,),
        in_specs=[pl.BlockSpec((bm, D_MODEL), lambda i: (i, 0)), _full((1, D_MODEL)),
                  _resident(wup.shape), _resident(wdown.shape)],
        out_specs=pl.BlockSpec((bm, D_MODEL), lambda i: (i, 0)),
        out_shape=jax.ShapeDtypeStruct((T, D_MODEL), F32),
        compiler_params=_params("parallel"),
        name="mlp",
    )(x2d, gain, wup, wdown)


def _ple_kernel(x_ref, p_ref, gain_ref, wg_ref, wp_ref, o_ref):
    x = x_ref[...]
    h = _rms(x, gain_ref[...]).astype(BF16)
    gate = jax.nn.sigmoid(_dot(h, wg_ref[...]))
    o_ref[...] = x + gate * _dot(p_ref[...].astype(BF16), wp_ref[...])


def _ple(x2d, p2d, gain, wg, wp, bm=1024):
    T = x2d.shape[0]
    return pl.pallas_call(
        _ple_kernel,
        grid=(T // bm,),
        in_specs=[pl.BlockSpec((bm, D_MODEL), lambda i: (i, 0)), pl.BlockSpec((bm, PLE_DIM), lambda i: (i, 0)),
                  _full((1, D_MODEL)), _resident(wg.shape), _resident(wp.shape)],
        out_specs=pl.BlockSpec((bm, D_MODEL), lambda i: (i, 0)),
        out_shape=jax.ShapeDtypeStruct((T, D_MODEL), F32),
        compiler_params=_params("parallel"),
        name="ple",
    )(x2d, p2d, gain, wg, wp)


def _rotate_half_cols(w):
    half = w.shape[-1] // 2
    return jnp.concatenate([-w[..., half:], w[..., :half]], axis=-1)


def _swap_halves(g):
    half = g.shape[-1] // 2
    return jnp.concatenate([g[..., half:], g[..., :half]], axis=-1)


def _layer(x2d, p2d, tables, B, S, norm_mix, w_in, ret_norm, q_lat_norm, kv_lat_norm, w_uq, w_ukv,
           q_norm, k_norm, w_br, w_bm, w_o, norm_mlp, w_up, w_down, norm_ple, w_ple_gate, w_ple):
    cos_r, sin_r, cos_m, sin_m = tables
    o_rv = 2 * RET_QK_W
    o_cq = o_rv + 2 * RET_V_W
    o_kr = o_cq + MLA_Q_LORA + MLA_KV_LORA
    o_gr = o_kr + MLA_ROPE_DIM

    w_proj = jnp.concatenate([w_in[:, :o_cq], w_in[:, o_gr:], w_in[:, o_cq:o_gr],
                              _rotate_half_cols(w_in[:, o_kr:o_gr])], axis=1).astype(BF16)
    qk, vg, cq, ckv, kr = _proj(x2d, norm_mix[None, :], w_proj, cos_r, sin_r,
                                q_lat_norm[None, :], kv_lat_norm[None, :])

    ro = _retention(qk, vg, ret_norm[None, :], B, S)

    wq = w_uq.reshape(MLA_Q_LORA, MLA_HEADS, MLA_QK_DIM)
    wq_rope = wq[:, :, MLA_NOPE_DIM:]
    wq = jnp.concatenate([wq, _rotate_half_cols(wq_rope)], axis=-1).reshape(MLA_Q_LORA, MLA_QK_W).astype(BF16)
    wkv = w_ukv.reshape(MLA_KV_LORA, MLA_HEADS, MLA_NOPE_DIM + MLA_V_DIM)
    wkn = wkv[:, :, :MLA_NOPE_DIM].reshape(MLA_KV_LORA, MLA_HEADS * MLA_NOPE_DIM).astype(BF16)
    wvt = wkv[:, :, MLA_NOPE_DIM:].reshape(MLA_KV_LORA, MLA_V_W).T.astype(BF16)
    gq = jnp.concatenate([q_norm, _swap_halves(q_norm[MLA_NOPE_DIM:])])[None, :]
    gkn = k_norm[None, :MLA_NOPE_DIM]
    gkr = jnp.concatenate([k_norm[MLA_NOPE_DIM:], _swap_halves(k_norm[MLA_NOPE_DIM:])])[None, :]
    q, k, vt = _mla_qkv(cq, ckv, kr, cos_m, sin_m, wq, wkn, wvt, gq, gkn, gkr)
    mo = _attention(q, k, vt, B, S)

    x2d = _merge(x2d, ro, mo, vg, w_br.astype(BF16), w_bm.astype(BF16), w_o.astype(BF16))
    x2d = _mlp(x2d, norm_mlp[None, :], w_up.astype(BF16), w_down.astype(BF16))
    x2d = _ple(x2d, p2d, norm_ple[None, :], w_ple_gate.astype(BF16), w_ple.astype(BF16))
    return x2d


def kernel(x, p, positions, norm_mix, w_in, ret_norm, q_lat_norm, kv_lat_norm, w_uq, w_ukv, q_norm, k_norm,
           w_br, w_bm, w_o, norm_mlp, w_up, w_down, norm_ple, w_ple_gate, w_ple):
    B, S, D = x.shape
    T = B * S
    x2d = x.reshape(T, D)
    pos_f = positions.astype(F32).reshape(T, 1)
    tables = _rope_tables(pos_f)
    for i in range(p.shape[0]):
        x2d = _layer(x2d, p[i].reshape(T, PLE_DIM), tables, B, S, norm_mix[i], w_in[i], ret_norm[i],
                     q_lat_norm[i], kv_lat_norm[i], w_uq[i], w_ukv[i], q_norm[i], k_norm[i], w_br[i], w_bm[i],
                     w_o[i], norm_mlp[i], w_up[i], w_down[i], norm_ple[i], w_ple_gate[i], w_ple[i])
    return x2d.reshape(B, S, D)
```

```python
import functools

import jax
import jax.numpy as jnp
from jax import lax
from jax.experimental import pallas as pl
from jax.experimental.pallas import tpu as pltpu

D_MODEL = 1024
PLE_DIM = 256
RET_HEADS = 4
RET_QK_DIM = 256
RET_V_DIM = 512
MLA_HEADS = 8
MLA_NOPE_DIM = 128
MLA_ROPE_DIM = 64
MLA_QK_DIM = MLA_NOPE_DIM + MLA_ROPE_DIM
MLA_V_DIM = 128
MLA_Q_LORA = 384
MLA_KV_LORA = 256
D_FF = 4 * D_MODEL
ROPE_BASE = 10000.0
RMS_EPS = 1e-6

RET_QK_W = RET_HEADS * RET_QK_DIM
RET_V_W = RET_HEADS * RET_V_DIM
MLA_V_W = MLA_HEADS * MLA_V_DIM
MLA_PAD_DIM = 256
MLA_QK_W = MLA_HEADS * MLA_PAD_DIM

LANES = 128
VMEM_LIMIT = 56 * 1024 * 1024
LOG2_E = 1.4426950408889634

F32 = jnp.float32
BF16 = jnp.bfloat16


def _params(*sem):
    return pltpu.CompilerParams(dimension_semantics=sem, vmem_limit_bytes=VMEM_LIMIT)


def _full(shape):
    nd = len(shape)
    return pl.BlockSpec(shape, lambda *_: (0,) * nd)


def _resident(shape):
    nd = len(shape)
    return pl.BlockSpec(shape, lambda *_: (0,) * nd, pipeline_mode=pl.Buffered(1))


def _rms(x, gain):
    ms = jnp.mean(x * x, axis=-1, keepdims=True)
    return x * lax.rsqrt(ms + RMS_EPS) * gain


def _dot(a, b):
    return jnp.dot(a, b, preferred_element_type=F32)


def _dot_nt(a, b):
    return lax.dot_general(a, b, (((1,), (1,)), ((), ())), preferred_element_type=F32)


def _dot_tn(a, b):
    return lax.dot_general(a, b, (((0,), (0,)), ((), ())), preferred_element_type=F32)


def _rope_freqs():
    half_r = RET_QK_DIM // 2
    half_m = MLA_ROPE_DIM // 2
    inv_r = ROPE_BASE ** (-jnp.arange(half_r, dtype=F32) / half_r)
    inv_m = ROPE_BASE ** (-jnp.arange(half_m, dtype=F32) / half_m)
    zeros = jnp.zeros((LANES - 2 * half_m,), F32)
    inv_m2 = jnp.concatenate([inv_m, inv_m, zeros])[None, :]
    mask = jnp.concatenate([jnp.ones((2 * half_m,), F32), zeros])[None, :]
    return inv_r[None, :], inv_m2, mask


PROJ_BLOCK = 1024
PROJ_VG_W =2 * RET_V_W + 2 * D_MODEL
PROJ_LAT_W = MLA_Q_LORA + MLA_KV_LORA + 2 * MLA_ROPE_DIM
PROJ_O_RV = 2 * RET_QK_W
PROJ_O_CQ = PROJ_O_RV + 2 * RET_V_W
PROJ_O_GR = PROJ_O_CQ + MLA_Q_LORA + MLA_KV_LORA + MLA_ROPE_DIM


def _proj_kernel(x_ref, gain_ref, wt_ref, wlat_ref, pos_ref, inv_r_ref, inv_m_ref, mask_ref, gq_ref, gkv_ref,
                 qk_ref, vg_ref, cq_ref, ckv_ref, kr_ref, cm_ref, sm_ref):
    h = _rms(x_ref[...], gain_ref[...]).astype(BF16)
    bn = PROJ_BLOCK
    for j in range(PROJ_VG_W // bn):
        src = PROJ_O_RV + j * bn if j * bn < 2 * RET_V_W else PROJ_O_GR + j * bn - 2 * RET_V_W
        acc = _dot_nt(h, wt_ref[src:src + bn, :])
        if j * bn >= 2 * RET_V_W:
            acc = jax.nn.sigmoid(acc)
        elif j * bn >= RET_V_W:
            acc = acc * jax.nn.sigmoid(acc)
        vg_ref[:, j * bn:(j + 1) * bn] = acc.astype(BF16)
    acc = _dot_nt(h, wlat_ref[...])
    cq_ref[...] = _rms(acc[:, :MLA_Q_LORA], gq_ref[...]).astype(BF16)
    ckv_ref[...] = _rms(acc[:, MLA_Q_LORA:MLA_Q_LORA + MLA_KV_LORA], gkv_ref[...]).astype(BF16)
    kr_ref[...] = acc[:, MLA_Q_LORA + MLA_KV_LORA:]

    pos = pos_ref[...]
    ang_m = pos * inv_m_ref[...]
    cm_ref[...] = jnp.cos(ang_m) * mask_ref[...]
    sm_ref[...] = jnp.sin(ang_m) * mask_ref[...]
    ang = pos * inv_r_ref[...]
    cos = jnp.cos(ang)
    sin = jnp.sin(ang)
    half = RET_QK_DIM // 2
    for j in range(2):
        acc = _dot_nt(h, wt_ref[j * RET_QK_W:(j + 1) * RET_QK_W, :])
        for hd in range(RET_HEADS):
            lo = hd * RET_QK_DIM
            x1 = acc[:, lo:lo + half]
            x2 = acc[:, lo + half:lo + 2 * half]
            r1 = x1 * cos - x2 * sin
            r2 = x2 * cos + x1 * sin
            if j == 1:
                r1 = r1 * (RET_QK_DIM ** -0.5)
                r2 = r2 * (RET_QK_DIM ** -0.5)
            qk_ref[:, j * RET_QK_W + lo:j * RET_QK_W + lo + half] = r1.astype(BF16)
            qk_ref[:, j * RET_QK_W + lo + half:j * RET_QK_W + lo + 2 * half] = r2.astype(BF16)


def _proj(x2d, pos_f, gain, wt, wlat, gq, gkv, bm=512):
    T = x2d.shape[0]
    inv_r, inv_m2, mask = _rope_freqs()

    def row(n):
        return pl.BlockSpec((bm, n), lambda i: (i, 0))

    return pl.pallas_call(
        _proj_kernel,
        grid=(T // bm,),
        in_specs=[row(D_MODEL), _full((1, D_MODEL)), _resident(wt.shape), _resident(wlat.shape),
                  row(1), _full((1, LANES)), _full((1, LANES)), _full((1, LANES)),
                  _full((1, MLA_Q_LORA)), _full((1, MLA_KV_LORA))],
        out_specs=[row(2 * RET_QK_W), row(PROJ_VG_W), row(MLA_Q_LORA), row(MLA_KV_LORA), row(2 * MLA_ROPE_DIM),
                   row(LANES), row(LANES)],
        out_shape=[
            jax.ShapeDtypeStruct((T, 2 * RET_QK_W), BF16),
            jax.ShapeDtypeStruct((T, PROJ_VG_W), BF16),
            jax.ShapeDtypeStruct((T, MLA_Q_LORA), BF16),
            jax.ShapeDtypeStruct((T, MLA_KV_LORA), BF16),
            jax.ShapeDtypeStruct((T, 2 * MLA_ROPE_DIM), F32),
            jax.ShapeDtypeStruct((T, LANES), F32),
            jax.ShapeDtypeStruct((T, LANES), F32),
        ],
        compiler_params=_params("parallel"),
        name="proj",
    )(x2d, gain, wt, wlat, pos_f, inv_r, inv_m2, mask, gq, gkv)


RET_HEADS_PER_STEP = 2
RET_KERNEL_CHUNK = 256


def _retention_kernel(q_ref, k_ref, v_ref, g_ref, dec_ref, qd_ref, kd_ref, cd_ref, gain_ref, o_ref, state_ref, *, tb):
    @pl.when(pl.program_id(2) == 0)
    def _():
        state_ref[...] = jnp.zeros_like(state_ref)

    C, dk, dv, hp = RET_KERNEL_CHUNK, RET_QK_DIM, RET_V_DIM, RET_HEADS_PER_STEP
    n = tb // C

    def intra_scores(hd, c):
        rows = slice(c * C, (c + 1) * C)
        cols = slice(hd * dk, (hd + 1) * dk)
        return _dot_nt(q_ref[rows, cols], k_ref[rows, cols])

    scores = {(hd, 0): intra_scores(hd, 0) for hd in range(hp)}
    for c in range(n):
        rows = slice(c * C, (c + 1) * C)
        for hd in range(hp):
            qcols = slice(hd * dk, (hd + 1) * dk)
            vcols = slice(hd * dv, (hd + 1) * dv)
            q = q_ref[rows, qcols]
            v = v_ref[rows, vcols]
            st = state_ref[hd]
            p = (scores.pop((hd, c)) * dec_ref[hd]).astype(BF16)
            o = _dot(p, v) + _dot(q, st.astype(BF16)) * qd_ref[hd]
            kk = (k_ref[rows, qcols].astype(F32) * kd_ref[hd]).astype(BF16)
            kv = _dot_tn(kk, v)
            if hd == hp - 1 and c + 1 < n:
                for h2 in range(hp):
                    scores[(h2, c + 1)] = intra_scores(h2, c + 1)
            state_ref[hd] = st * cd_ref[hd, 0:1, 0:1] + kv
            on = _rms(o, gain_ref[:, vcols])
            o_ref[rows, vcols] = (on * g_ref[rows, vcols].astype(F32)).astype(BF16)


def _retention(qk, vg, ret_gain, B, S, tb=1024):
    T = qk.shape[0]
    H, C, dk, dv, hp = RET_HEADS, RET_KERNEL_CHUNK, RET_QK_DIM, RET_V_DIM, RET_HEADS_PER_STEP
    nt = S // tb
    log_g = jnp.log1p(-jnp.exp2(-5.0 - jnp.arange(H, dtype=F32)))
    idx = jnp.arange(C, dtype=F32)
    diff = idx[:, None] - idx[None, :]
    decay_in = jnp.where(diff >= 0, jnp.exp(jnp.maximum(diff, 0.0)[None] * log_g[:, None, None]), 0.0)
    q_dec = jnp.exp((idx + 1.0)[None, :] * log_g[:, None])
    k_dec = jnp.exp((C - 1.0 - idx)[None, :] * log_g[:, None])
    chunk_dec = jnp.exp(C * log_g)
    qd = jnp.broadcast_to(q_dec[:, :, None], (H, C, dv))
    kd = jnp.broadcast_to(k_dec[:, :, None], (H, C, dk))
    cd = jnp.broadcast_to(chunk_dec[:, None, None], (H, 8, LANES))

    def tab(d1, d2):
        return pl.BlockSpec((hp, d1, d2), lambda b, h, t: (h, 0, 0))

    ng = H // hp
    return pl.pallas_call(
        functools.partial(_retention_kernel, tb=tb),
        grid=(B, ng, nt),
        in_specs=[
            pl.BlockSpec((tb, hp * dk), lambda b, h, t: (b * nt + t, h)),
            pl.BlockSpec((tb, hp * dk), lambda b, h, t: (b * nt + t, ng + h)),
            pl.BlockSpec((tb, hp * dv), lambda b, h, t: (b * nt + t, h)),
            pl.BlockSpec((tb, hp * dv), lambda b, h, t: (b * nt + t, ng + h)),
            tab(C, C), tab(C, dv), tab(C, dk), tab(8, LANES),
            pl.BlockSpec((1, hp * dv), lambda b, h, t: (0, h)),
        ],
        out_specs=pl.BlockSpec((tb, hp * dv), lambda b, h, t: (b * nt + t, h)),
        out_shape=jax.ShapeDtypeStruct((T, RET_V_W), BF16),
        scratch_shapes=[pltpu.VMEM((hp, dk, dv), F32)],
        compiler_params=_params("parallel", "parallel", "arbitrary"),
        name="retention",
    )(qk, qk, vg, vg, decay_in, qd, kd, cd, ret_gain)


def _mla_qkv_kernel(cq_ref, ckv_ref, kr_ref, cm_ref, sm_ref, wq_ref, wkn_ref, wvt_ref,
                    gq_ref, gkn_ref, gkr_ref, q_ref, k_ref, vt_ref):
    ckv = ckv_ref[...]
    yq = _dot(cq_ref[...], wq_ref[...])
    ykn = _dot(ckv, wkn_ref[...])
    vt_ref[...] = _dot_nt(wvt_ref[...], ckv).astype(BF16)
    cm = cm_ref[...]
    sm = sm_ref[...]
    first = (lax.broadcasted_iota(jnp.int32, (1, LANES), 1) < MLA_ROPE_DIM).astype(F32)
    inv_d = 1.0 / MLA_QK_DIM
    half = MLA_PAD_DIM // 2

    kr = kr_ref[...]
    kr_sq = jnp.square(kr * first)
    krg = kr * gkr_ref[...]
    k_roped = krg * cm + pltpu.roll(krg, MLA_ROPE_DIM, 1) * sm

    gq1 = gq_ref[:, :half]
    gq2 = gq_ref[:, half:]
    gkn = gkn_ref[...]
    for h in range(MLA_HEADS):
        lo = h * MLA_PAD_DIM
        y1 = yq[:, lo:lo + half]
        y2 = yq[:, lo + half:lo + 2 * half]
        ss = jnp.sum(y1 * y1 + jnp.square(y2 * first), axis=-1, keepdims=True)
        sc = lax.rsqrt(ss * inv_d + RMS_EPS) * (MLA_QK_DIM ** -0.5 * LOG2_E)
        y2g = y2 * gq2 * sc
        q_ref[:, lo:lo + half] = (y1 * gq1 * sc).astype(BF16)
        q_ref[:, lo + half:lo + 2 * half] = (y2g * cm + pltpu.roll(y2g, MLA_ROPE_DIM, 1) * sm).astype(BF16)
        kn = ykn[:, h * half:(h + 1) * half]
        ssk = jnp.sum(kn * kn + kr_sq, axis=-1, keepdims=True)
        sck = lax.rsqrt(ssk * inv_d + RMS_EPS)
        k_ref[:, lo:lo + half] = (kn * gkn * sck).astype(BF16)
        k_ref[:, lo + half:lo + 2 * half] = (k_roped * sck).astype(BF16)


def _mla_qkv(cq, ckv, kr, cm, sm, wq, wkn, wvt, gq, gkn, gkr, bm=512):
    T = cq.shape[0]

    def row(w):
        return pl.BlockSpec((bm, w), lambda i: (i, 0))

    return pl.pallas_call(
        _mla_qkv_kernel,
        grid=(T // bm,),
        in_specs=[row(MLA_Q_LORA), row(MLA_KV_LORA), row(LANES), row(LANES), row(LANES),
                  _full(wq.shape), _full(wkn.shape), _full(wvt.shape),
                  _full(gq.shape), _full(gkn.shape), _full(gkr.shape)],
        out_specs=[row(MLA_QK_W), row(MLA_QK_W), pl.BlockSpec((MLA_V_W, bm), lambda i: (0, i))],
        out_shape=[jax.ShapeDtypeStruct((T, MLA_QK_W), BF16),
                   jax.ShapeDtypeStruct((T, MLA_QK_W), BF16),
                   jax.ShapeDtypeStruct((MLA_V_W, T), BF16)],
        compiler_params=_params("parallel"),
        name="mla_qkv",
    )(cq, ckv, kr, cm, sm, wq, wkn, wvt, gq, gkn, gkr)


ATTN_HEADS_PER_STEP = 2


def _attn_kernel(q_ref, k_ref, vt_ref, o_ref, m_ref, l_ref, acc_ref, s_ref, bm_ref, *, tq):
    qi = pl.program_id(2)
    m_ref[...] = jnp.full_like(m_ref, -1e30)
    l_ref[...] = jnp.zeros_like(l_ref)
    acc_ref[...] = jnp.zeros_like(acc_ref)

    def scores(c, kj):
        r0 = pl.multiple_of(kj * tq, tq)
        cols = slice(c * MLA_PAD_DIM, (c + 1) * MLA_PAD_DIM)
        return _dot_nt(k_ref[pl.ds(r0, tq), cols], q_ref[:, cols])

    def park(kj):
        st = scores(0, kj)
        s_ref[...] = st
        bm_ref[...] = jnp.max(st, axis=0, keepdims=True)

    def update(c, kj, st, masked, block_max=None):
        r0 = pl.multiple_of(kj * tq, tq)
        if masked:
            kpos = lax.broadcasted_iota(jnp.int32, (tq, tq), 0)
            qpos = lax.broadcasted_iota(jnp.int32, (tq, tq), 1)
            st = jnp.where(kpos <= qpos, st, -1e30)
        if block_max is None:
            block_max = jnp.max(st, axis=0, keepdims=True)
        rows = slice(c * MLA_V_DIM, (c + 1) * MLA_V_DIM)
        m_prev = m_ref[c:c + 1, :]
        m_new = jnp.maximum(m_prev, block_max)
        alpha = jnp.exp2(m_prev - m_new)
        pt = jnp.exp2(st - m_new)
        l_ref[c:c + 1, :] = alpha * l_ref[c:c + 1, :] + jnp.sum(pt, axis=0, keepdims=True)
        acc_ref[rows, :] = alpha * acc_ref[rows, :] + _dot(vt_ref[rows, pl.ds(r0, tq)], pt.astype(BF16))
        m_ref[c:c + 1, :] = m_new

    def step(kj, masked):
        s1 = scores(1, kj)
        update(0, kj, s_ref[...], masked, None if masked else bm_ref[...])
        if not masked:
            park(kj + 1)
        update(1, kj, s1, masked)

    def body(kj, carry):
        step(kj, False)
        return carry

    park(0)
    lax.fori_loop(0, qi, body, 0)
    step(qi, True)
    for c in range(ATTN_HEADS_PER_STEP):
        rows = slice(c * MLA_V_DIM, (c + 1) * MLA_V_DIM)
        o_ref[rows, :] = (acc_ref[rows, :] / l_ref[c:c + 1, :]).astype(BF16)


def _attention(q, k, vt, B, S, tq=512):
    T = q.shape[0]
    nq = S // tq
    hp = ATTN_HEADS_PER_STEP
    return pl.pallas_call(
        functools.partial(_attn_kernel, tq=tq),
        grid=(B, MLA_HEADS // hp, nq),
        in_specs=[
            pl.BlockSpec((tq, hp * MLA_PAD_DIM), lambda b, h, i: (b * nq + i, h)),
            pl.BlockSpec((S, hp * MLA_PAD_DIM), lambda b, h, i: (b, h)),
            pl.BlockSpec((hp * MLA_V_DIM, S), lambda b, h, i: (h, b)),
        ],
        out_specs=pl.BlockSpec((hp * MLA_V_DIM, tq), lambda b, h, i: (h, b * nq + i)),
        out_shape=jax.ShapeDtypeStruct((MLA_V_W, T), BF16),
        scratch_shapes=[pltpu.VMEM((hp, tq), F32), pltpu.VMEM((hp, tq), F32),
                        pltpu.VMEM((hp * MLA_V_DIM, tq), F32), pltpu.VMEM((tq, tq), F32),
                        pltpu.VMEM((1, tq), F32)],
        compiler_params=_params("parallel", "parallel", "arbitrary"),
        name="mla_attention",
    )(q, k, vt)


def _merge_kernel(x_ref, ro_ref, mot_ref, gr_ref, gm_ref, wbr_ref, wbm_ref, wo_ref, o_ref):
    a_ret = _dot(ro_ref[...], wbr_ref[...])
    a_mla = _dot_tn(mot_ref[...], wbm_ref[...])
    mixed = gr_ref[...].astype(F32) * a_ret + gm_ref[...].astype(F32) * a_mla
    o_ref[...] = x_ref[...] + _dot(mixed.astype(BF16), wo_ref[...])


def _merge(x2d, ro, mo, vg, wbr, wbm, wo, bm=512):
    T = x2d.shape[0]
    gate0 = (2 * RET_V_W) // D_MODEL
    return pl.pallas_call(
        _merge_kernel,
        grid=(T // bm,),
        in_specs=[
            pl.BlockSpec((bm, D_MODEL), lambda i: (i, 0)),
            pl.BlockSpec((bm, RET_V_W), lambda i: (i, 0)),
            pl.BlockSpec((MLA_V_W, bm), lambda i: (0, i)),
            pl.BlockSpec((bm, D_MODEL), lambda i: (i, gate0)),
            pl.BlockSpec((bm, D_MODEL), lambda i: (i, gate0 + 1)),
            _resident(wbr.shape), _resident(wbm.shape), _resident(wo.shape),
        ],
        out_specs=pl.BlockSpec((bm, D_MODEL), lambda i: (i, 0)),
        out_shape=jax.ShapeDtypeStruct((T, D_MODEL), F32),
        compiler_params=_params("parallel"),
        name="merge_out_proj",
    )(x2d, ro, mo, vg, vg, wbr, wbm, wo)


def _mlp_kernel(x_ref, gain_ref, wup_ref, wdown_ref, o_ref, *, bf):
    x = x_ref[...]
    h = _rms(x, gain_ref[...]).astype(BF16)
    acc = x
    for c in range(D_FF // bf):
        up = _dot(h, wup_ref[:, c * bf:(c + 1) * bf])
        a = jnp.square(jnp.maximum(up, 0.0)).astype(BF16)
        acc = acc + _dot(a, wdown_ref[c * bf:(c + 1) * bf, :])
    o_ref[...] = acc


def _mlp(x2d, gain, wup, wdown, bm=512, bf=1024):
    T = x2d.shape[0]
    return pl.pallas_call(
        functools.partial(_mlp_kernel, bf=bf),
        grid=(T // bm,),
        in_specs=[pl.BlockSpec((bm, D_MODEL), lambda i: (i, 0)), _full((1, D_MODEL)),
                  _resident(wup.shape), _resident(wdown.shape)],
        out_specs=pl.BlockSpec((bm, D_MODEL), lambda i: (i, 0)),
        out_shape=jax.ShapeDtypeStruct((T, D_MODEL), F32),
        compiler_params=_params("parallel"),
        name="mlp",
    )(x2d, gain, wup, wdown)


def _ple_kernel(x_ref, p_ref, gain_ref, wg_ref, wp_ref, o_ref):
    x = x_ref[...]
    h = _rms(x, gain_ref[...]).astype(BF16)
    gate = jax.nn.sigmoid(_dot(h, wg_ref[...]))
    o_ref[...] = x + gate * _dot(p_ref[...].astype(BF16), wp_ref[...])


def _ple(x2d, p2d, gain, wg, wp, bm=1024):
    T = x2d.shape[0]
    return pl.pallas_call(
        _ple_kernel,
        grid=(T // bm,),
        in_specs=[pl.BlockSpec((bm, D_MODEL), lambda i: (i, 0)), pl.BlockSpec((bm, PLE_DIM), lambda i: (i, 0)),
                  _full((1, D_MODEL)), _resident(wg.shape), _resident(wp.shape)],
        out_specs=pl.BlockSpec((bm, D_MODEL), lambda i: (i, 0)),
        out_shape=jax.ShapeDtypeStruct((T, D_MODEL), F32),
        compiler_params=_params("parallel"),
        name="ple",
    )(x2d, p2d, gain, wg, wp)


def _rotate_half_cols(w):
    half = w.shape[-1] // 2
    return jnp.concatenate([-w[..., half:], w[..., :half]], axis=-1)


def _swap_halves(g):
    half = g.shape[-1] // 2
    return jnp.concatenate([g[..., half:], g[..., :half]], axis=-1)


def _layer(x2d, p2d, pos_f, B, S, norm_mix, w_in, ret_norm, q_lat_norm, kv_lat_norm, w_uq, w_ukv,
           q_norm, k_norm, w_br, w_bm, w_o, norm_mlp, w_up, w_down, norm_ple, w_ple_gate, w_ple):
    wt = w_in.T.astype(BF16)
    o_kr = PROJ_O_GR - MLA_ROPE_DIM
    wlat = jnp.concatenate([wt[PROJ_O_CQ:PROJ_O_GR], _rotate_half_cols(w_in[:, o_kr:PROJ_O_GR]).T.astype(BF16)])
    qk, vg, cq, ckv, kr, cos_m, sin_m = _proj(x2d, pos_f, norm_mix[None, :], wt, wlat,
                                              q_lat_norm[None, :], kv_lat_norm[None, :])

    ro = _retention(qk, vg, ret_norm[None, :], B, S)

    wq = w_uq.reshape(MLA_Q_LORA, MLA_HEADS, MLA_QK_DIM)
    wq_rope = wq[:, :, MLA_NOPE_DIM:]
    wq = jnp.concatenate([wq, _rotate_half_cols(wq_rope)], axis=-1).reshape(MLA_Q_LORA, MLA_QK_W).astype(BF16)
    wkv = w_ukv.reshape(MLA_KV_LORA, MLA_HEADS, MLA_NOPE_DIM + MLA_V_DIM)
    wkn = wkv[:, :, :MLA_NOPE_DIM].reshape(MLA_KV_LORA, MLA_HEADS * MLA_NOPE_DIM).astype(BF16)
    wvt = wkv[:, :, MLA_NOPE_DIM:].reshape(MLA_KV_LORA, MLA_V_W).T.astype(BF16)
    gq = jnp.concatenate([q_norm, _swap_halves(q_norm[MLA_NOPE_DIM:])])[None, :]
    gkn = k_norm[None, :MLA_NOPE_DIM]
    gkr = jnp.concatenate([k_norm[MLA_NOPE_DIM:], _swap_halves(k_norm[MLA_NOPE_DIM:])])[None, :]
    q, k, vt = _mla_qkv(cq, ckv, kr, cos_m, sin_m, wq, wkn, wvt, gq, gkn, gkr)
    mo = _attention(q, k, vt, B, S)

    x2d = _merge(x2d, ro, mo, vg, w_br.astype(BF16), w_bm.astype(BF16), w_o.astype(BF16))
    x2d = _mlp(x2d, norm_mlp[None, :], w_up.astype(BF16), w_down.astype(BF16))
    x2d = _ple(x2d, p2d, norm_ple[None, :], w_ple_gate.astype(BF16), w_ple.astype(BF16))
    return x2d


def kernel(x, p, positions, norm_mix, w_in, ret_norm, q_lat_norm, kv_lat_norm, w_uq, w_ukv, q_norm, k_norm,
           w_br, w_bm, w_o, norm_mlp, w_up, w_down, norm_ple, w_ple_gate, w_ple):
    B, S, D = x.shape
    T = B * S
    x2d = x.reshape(T, D)
    pos_f = positions.astype(F32).reshape(T, 1)
    for i in range(p.shape[0]):
        x2d = _layer(x2d, p[i].reshape(T, PLE_DIM), pos_f, B, S, norm_mix[i], w_in[i], ret_norm[i],
                     q_lat_norm[i], kv_lat_norm[i], w_uq[i], w_ukv[i], q_norm[i], k_norm[i], w_br[i], w_bm[i],
                     w_o[i], norm_mlp[i], w_up[i], w_down[i], norm_ple[i], w_ple_gate[i], w_ple[i])
    return x2d.reshape(B, S, D)
```
